```python
import jax, jax.numpy as jnp
from jax import lax
import numpy as np

D_MODEL = 1024
BATCH = 16
SEQ = 4096
DEPTH = 4

CHUNK = 128
A_WIDTH = D_MODEL
A_GROUPS = 8
A_GROUP_DIM = A_WIDTH // A_GROUPS
HEAD_DIM = 64
N_HEADS = D_MODEL // HEAD_DIM
N_KV = 4
GQ = N_HEADS // N_KV
Q_WIDTH = N_HEADS * HEAD_DIM
KV_WIDTH = N_KV * HEAD_DIM
BLOCK = 128
WINDOW = 128
D_FF = 4 * D_MODEL
EPS = 1e-6
SECTION_SIZES = [D_MODEL, D_MODEL, 2 * A_WIDTH, Q_WIDTH, KV_WIDTH, KV_WIDTH]
SPLITS = [int(s) for s in np.cumsum(SECTION_SIZES)[:-1]]
IN_WIDTH = int(sum(SECTION_SIZES))

kernel_name = "hybrid_gmlp_swa_alibi_encoder"


def _rmsnorm(x, g):
    xf = x.astype(jnp.float32)
    y = xf * lax.rsqrt(jnp.mean(xf * xf, axis=-1, keepdims=True) + EPS)
    return (y * g.astype(jnp.float32)).astype(x.dtype)


def _spatial_gating(za, g_v, w_s, b_s):
    B, S, _ = za.shape
    za = jax.nn.gelu(za, approximate=False)
    u, v = jnp.split(za, 2, axis=-1)
    v = _rmsnorm(v, g_v)
    nc = S // CHUNK
    v = v.reshape(B, nc, CHUNK, A_GROUPS, A_GROUP_DIM)
    s = jnp.einsum('gpq,bnqgc->bnpgc', w_s, v) + b_s.T[None, None, :, :, None]
    return u * s.reshape(B, S, A_WIDTH)


def _window_attention(q, k, v, g_q, g_k, sink):
    B, S, _ = q.shape
    nb = S // BLOCK
    f32 = jnp.float32
    q = _rmsnorm(q.reshape(B, S, N_HEADS, HEAD_DIM), g_q)
    k = _rmsnorm(k.reshape(B, S, N_KV, HEAD_DIM), g_k)
    v = v.reshape(B, S, N_KV, HEAD_DIM)
    qb = q.reshape(B, nb, BLOCK, N_KV, GQ, HEAD_DIM).transpose(1, 0, 2, 3, 4, 5)

    def band(t):
        tb = t.reshape(B, nb, BLOCK, N_KV, HEAD_DIM)
        tp = jnp.pad(tb, ((0, 0), (1, 1), (0, 0), (0, 0), (0, 0)))
        w = jnp.concatenate([tp[:, :-2], tp[:, 1:-1], tp[:, 2:]], axis=2)
        return w.transpose(1, 0, 2, 3, 4)

    kw, vw = band(k), band(v)
    qi = jnp.arange(BLOCK)[:, None]
    kj = jnp.arange(3 * BLOCK)[None, :]
    rel = kj - BLOCK - qi
    in_window = jnp.abs(rel) <= WINDOW
    slopes = jnp.exp2(-8.0 * (jnp.arange(N_HEADS, dtype=f32) + 1.0) / N_HEADS).reshape(N_KV, GQ)
    alibi = -slopes[:, :, None, None] * jnp.abs(rel).astype(f32)[None, None]
    sink_f = sink.astype(f32).reshape(N_KV, GQ)[None, :, :, None]
    scale = HEAD_DIM ** -0.5
    neg = jnp.float32(-1e30)

    def one_block(args):
        qblk, kblk, vblk, i = args
        s = jnp.einsum('bqkgd,bskd->bkgqs', qblk.astype(f32), kblk.astype(f32)) * scale + alibi
        kpos = (i - 1) * BLOCK + kj
        valid = in_window & (kpos >= 0) & (kpos < S)
        s = jnp.where(valid, s, neg)
        m = jnp.maximum(jnp.max(s, axis=-1), sink_f)
        p = jnp.exp(s - m[..., None])
        denom = jnp.sum(p, axis=-1) + jnp.exp(sink_f - m)
        o = jnp.einsum('bkgqs,bskd->bqkgd', p, vblk.astype(f32))
        o = o / denom.transpose(0, 3, 1, 2)[..., None]
        return o.astype(qblk.dtype)

    o = lax.map(one_block, (qb, kw, vw, jnp.arange(nb)))
    return o.transpose(1, 0, 2, 3, 4, 5).reshape(B, S, Q_WIDTH)


def setup_inputs(seed: int = 0) -> dict:
    key = jax.random.key(seed)
    ks = jax.random.split(key, 16)
    L = DEPTH
    nrm = lambda k, shape, fan: jax.random.normal(k, shape, jnp.float32) * (fan ** -0.5)
    gain = lambda k, shape: 1.0 + 0.02 * jax.random.normal(k, shape, jnp.float32)
    return {
        "x": jax.random.normal(ks[0], (BATCH, SEQ, D_MODEL), jnp.float32),
        "ln1_g": gain(ks[1], (L, D_MODEL)),
        "w_in": nrm(ks[2], (L, D_MODEL, IN_WIDTH), D_MODEL),
        "a_norm_g": gain(ks[3], (L, A_WIDTH)),
        "a_w_s": nrm(ks[4], (L, A_GROUPS, CHUNK, CHUNK), CHUNK),
        "a_b_s": 1.0 + 0.01 * jax.random.normal(ks[5], (L, A_GROUPS, CHUNK), jnp.float32),
        "b_q_norm_g": gain(ks[6], (L, HEAD_DIM)),
        "b_k_norm_g": gain(ks[7], (L, HEAD_DIM)),
        "b_sink": 0.5 * jax.random.normal(ks[8], (L, N_HEADS), jnp.float32),
        "w_branch_a": nrm(ks[9], (L, A_WIDTH, D_MODEL), A_WIDTH),
        "w_branch_b": nrm(ks[10], (L, Q_WIDTH, D_MODEL), Q_WIDTH),
        "w_out": nrm(ks[11], (L, D_MODEL, D_MODEL), D_MODEL),
        "ln2_g": gain(ks[12], (L, D_MODEL)),
        "w_ff1": nrm(ks[13], (L, D_MODEL, D_FF), D_MODEL),
        "w_ff2": nrm(ks[14], (L, D_FF, D_MODEL), D_FF),
    }


def reference(x, ln1_g, w_in, a_norm_g, a_w_s, a_b_s, b_q_norm_g, b_k_norm_g, b_sink,
              w_branch_a, w_branch_b, w_out, ln2_g, w_ff1, w_ff2):
    for l in range(DEPTH):
        h = _rmsnorm(x, ln1_g[l])
        z = h @ w_in[l]
        g_a, g_b, za, q, k, v = jnp.split(z, SPLITS, axis=-1)
        y_a = _spatial_gating(za, a_norm_g[l], a_w_s[l], a_b_s[l]) @ w_branch_a[l]
        y_b = _window_attention(q, k, v, b_q_norm_g[l], b_k_norm_g[l], b_sink[l]) @ w_branch_b[l]
        mixed = jax.nn.sigmoid(g_a) * y_a + jax.nn.sigmoid(g_b) * y_b
        x = x + mixed @ w_out[l]
        h = _rmsnorm(x, ln2_g[l])
        x = x + jnp.square(jax.nn.relu(h @ w_ff1[l])) @ w_ff2[l]
    return x
```

```python
import functools

import jax
import jax.numpy as jnp
import numpy as np
from jax import lax
from jax.experimental import pallas as pl
from jax.experimental.pallas import tpu as pltpu

D_MODEL = 1024
CHUNK = 128
A_GROUPS = 8
HEAD_DIM = 64
N_HEADS = 16
N_KV = 4
GQ = N_HEADS // N_KV
WINDOW = 128
D_FF = 4 * D_MODEL
EPS = 1e-6
LANES = 128
MASKED_DISTANCE = 1e30

VMEM_LIMIT_BYTES = 56 * 1024 * 1024

TOKENS_KV = 1024
TOKENS_MIX = 512
TOKENS_FFN = 512
FF_CHUNK = 1024

_SLOPES = [float(2.0 ** (-8.0 * (h + 1.0) / N_HEADS)) for h in range(N_HEADS)]


def _const_spec(shape):
    nd = len(shape)
    return pl.BlockSpec(shape, lambda *_: (0,) * nd, pipeline_mode=pl.Buffered(1))


def _rmsnorm_rows(x, g):
    ms = jnp.mean(x * x, axis=-1, keepdims=True)
    return x * lax.rsqrt(ms + EPS) * g


def _gelu_exact(x):
    return 0.5 * x * (1.0 + lax.erf(x * np.float32(1.0 / np.sqrt(2.0))))


def _dot(a, b):
    return jnp.dot(a, b, preferred_element_type=jnp.float32)


def _kv_proj_kernel(x_ref, ln_ref, w_ref, gk_ref, k_ref, v_ref):
    h = _rmsnorm_rows(x_ref[...], ln_ref[...]).astype(jnp.bfloat16)
    kv = _dot(h, w_ref[...])
    kw = N_KV * LANES
    parts = []
    for g in range(N_KV):
        kg = kv[:, g * LANES:(g + 1) * LANES]
        parts.append(_rmsnorm_rows(kg, gk_ref[:, g * LANES:(g + 1) * LANES]))
    k_ref[...] = jnp.concatenate(parts, axis=-1).astype(jnp.bfloat16)
    v_ref[...] = kv[:, kw:].astype(jnp.bfloat16)


def _kv_proj(x2, ln_g, w_kv, gk):
    t = x2.shape[0]
    tm = min(TOKENS_KV, t)
    kw = N_KV * LANES
    return pl.pallas_call(
        _kv_proj_kernel,
        grid=(t // tm,),
        in_specs=[
            pl.BlockSpec((tm, D_MODEL), lambda i: (i, 0)),
            _const_spec((1, D_MODEL)),
            _const_spec((D_MODEL, 2 * kw)),
            _const_spec((1, kw)),
        ],
        out_specs=[
            pl.BlockSpec((tm, kw), lambda i: (i, 0)),
            pl.BlockSpec((tm, kw), lambda i: (i, 0)),
        ],
        out_shape=[
            jax.ShapeDtypeStruct((t, kw), jnp.bfloat16),
            jax.ShapeDtypeStruct((t, kw), jnp.bfloat16),
        ],
        compiler_params=pltpu.CompilerParams(
            dimension_semantics=("arbitrary",), vmem_limit_bytes=VMEM_LIMIT_BYTES),
        name="kv_proj",
    )(x2, ln_g, w_kv, gk)


def _gmlp_kernel(x_ref, ln_ref, w_ref, gv_ref, ws_ref, bs_ref, wa_ref, o_ref, gated_ref):
    rows = x_ref.shape[0]
    h = _rmsnorm_rows(x_ref[...], ln_ref[...]).astype(jnp.bfloat16)
    u = _gelu_exact(_dot(h, w_ref[:, D_MODEL:2 * D_MODEL]))
    v = _gelu_exact(_dot(h, w_ref[:, 2 * D_MODEL:]))
    vn = _rmsnorm_rows(v, gv_ref[...]).astype(jnp.bfloat16)
    for n in range(rows // CHUNK):
        r0 = n * CHUNK
        for g in range(A_GROUPS):
            c0 = g * LANES
            s = _dot(ws_ref[g], vn[r0:r0 + CHUNK, c0:c0 + LANES]) + bs_ref[g]
            gated_ref[r0:r0 + CHUNK, c0:c0 + LANES] = (
                u[r0:r0 + CHUNK, c0:c0 + LANES] * s).astype(jnp.bfloat16)
    ya = _dot(gated_ref[...], wa_ref[...])
    ga = _dot(h, w_ref[:, :D_MODEL])
    o_ref[...] = (jax.nn.sigmoid(ga) * ya).astype(jnp.bfloat16)


def _gmlp(x2, ln_g, w_a, gv, ws, bs, wa):
    t = x2.shape[0]
    tm = TOKENS_MIX
    return pl.pallas_call(
        _gmlp_kernel,
        grid=(t // tm,),
        in_specs=[
            pl.BlockSpec((tm, D_MODEL), lambda i: (i, 0)),
            _const_spec((1, D_MODEL)),
            _const_spec((D_MODEL, 3 * D_MODEL)),
            _const_spec((1, D_MODEL)),
            _const_spec((A_GROUPS, CHUNK, CHUNK)),
            _const_spec((A_GROUPS, CHUNK, LANES)),
            _const_spec((D_MODEL, D_MODEL)),
        ],
        out_specs=pl.BlockSpec((tm, D_MODEL), lambda i: (i, 0)),
        out_shape=jax.ShapeDtypeStruct((t, D_MODEL), jnp.bfloat16),
        scratch_shapes=[pltpu.VMEM((tm, D_MODEL), jnp.bfloat16)],
        compiler_params=pltpu.CompilerParams(
            dimension_semantics=("arbitrary",), vmem_limit_bytes=VMEM_LIMIT_BYTES),
        name="gmlp",
    )(x2, ln_g, w_a, gv, ws, bs, wa)


def _attn_mix_kernel(x_ref, ya_ref, kp_ref, kc_ref, kn_ref, vp_ref, vc_ref, vn_ref,
                     ln_ref, w_ref, gq_ref, sink_ref, dist_ref, wb_ref, wo_ref,
                     o_ref, attn_ref):
    rows = x_ref.shape[0]
    nblk = rows // CHUNK
    i = pl.program_id(1)
    last = pl.num_programs(1) - 1

    x = x_ref[...]
    h = _rmsnorm_rows(x, ln_ref[...]).astype(jnp.bfloat16)
    q = _dot(h, w_ref[:, D_MODEL:])

    lane = lax.broadcasted_iota(jnp.int32, (rows, LANES), 1)
    low = lane < HEAD_DIM
    q_heads = []
    for pp in range(N_HEADS // 2):
        t = q[:, pp * LANES:(pp + 1) * LANES]
        t2 = t * t
        ssq_all = jnp.sum(t2, axis=-1, keepdims=True)
        ssq_low = jnp.sum(jnp.where(low, t2, 0.0), axis=-1, keepdims=True)
        r_low = lax.rsqrt(ssq_low * (1.0 / HEAD_DIM) + EPS)
        r_high = lax.rsqrt((ssq_all - ssq_low) * (1.0 / HEAD_DIM) + EPS)
        qn = t * jnp.where(low, r_low, r_high) * gq_ref[:, pp * LANES:(pp + 1) * LANES]
        q_heads.append(jnp.where(low, qn, 0.0).astype(jnp.bfloat16))
        q_heads.append(jnp.where(low, 0.0, qn).astype(jnp.bfloat16))

    k_ext = jnp.concatenate([kp_ref[...], kc_ref[...], kn_ref[...]], axis=0)
    v_ext = jnp.concatenate([vp_ref[...], vc_ref[...], vn_ref[...]], axis=0)
    ones = jnp.ones((3 * CHUNK, LANES), jnp.bfloat16)
    lane_blk = lax.broadcasted_iota(jnp.int32, (CHUNK, LANES), 1)
    low_blk = lane_blk < HEAD_DIM

    for jj in range(nblk):
        r0 = jj * CHUNK
        variant = jnp.int32(0)
        if jj == 0:
            variant = jnp.where(i == 0, 1, variant)
        if jj == nblk - 1:
            variant = jnp.where(i == last, 2, variant)
        dist = dist_ref[variant]
        for g in range(N_KV):
            lhs = jnp.concatenate(
                [q_heads[GQ * g + hh][r0:r0 + CHUNK] for hh in range(GQ)], axis=0)
            kd = k_ext[r0:r0 + 3 * CHUNK, g * LANES:(g + 1) * LANES]
            s = lax.dot_general(lhs, kd, (((1,), (1,)), ((), ())),
                                preferred_element_type=jnp.float32)
            ps, ms = [], []
            for hh in range(GQ):
                head = GQ * g + hh
                sh = s[hh * CHUNK:(hh + 1) * CHUNK] - _SLOPES[head] * dist
                sink = sink_ref[:, head:head + 1]
                m = jnp.maximum(jnp.max(sh, axis=-1, keepdims=True), sink)
                ps.append(jnp.exp(sh - m).astype(jnp.bfloat16))
                ms.append(jnp.exp(sink - m))
            p = jnp.concatenate(ps, axis=0)
            vd = jnp.concatenate(
                [v_ext[r0:r0 + 3 * CHUNK, g * LANES:(g + 1) * LANES], ones], axis=1)
            r = _dot(p, vd)
            outs = []
            for hh in range(GQ):
                rh = r[hh * CHUNK:(hh + 1) * CHUNK]
                outs.append(rh[:, :LANES] / (rh[:, LANES:] + ms[hh]))
            for pr in range(GQ // 2):
                pair = jnp.where(low_blk, outs[2 * pr], outs[2 * pr + 1])
                c0 = (g * (GQ // 2) + pr) * LANES
                attn_ref[r0:r0 + CHUNK, c0:c0 + LANES] = pair.astype(jnp.bfloat16)

    yb = _dot(attn_ref[...], wb_ref[...])
    gb = _dot(h, w_ref[:, :D_MODEL])
    mixed = (jax.nn.sigmoid(gb) * yb + ya_ref[...].astype(jnp.float32)).astype(jnp.bfloat16)
    o_ref[...] = x + _dot(mixed, wo_ref[...])


def _attn_mix(x3, ya3, kk3, vv3, ln_g, w_b, gq, sink, dist, wb, wo):
    b, s, _ = x3.shape
    tm = TOKENS_MIX
    nblk = tm // CHUNK
    nb = s // CHUNK
    kw = N_KV * LANES
    tile = lambda bi, i: (bi, i, 0)
    prev = lambda bi, i: (bi, jnp.maximum(i * nblk - 1, 0), 0)
    nxt = lambda bi, i: (bi, jnp.minimum((i + 1) * nblk, nb - 1), 0)
    return pl.pallas_call(
        _attn_mix_kernel,
        grid=(b, s // tm),
        in_specs=[
            pl.BlockSpec((None, tm, D_MODEL), tile),
            pl.BlockSpec((None, tm, D_MODEL), tile),
            pl.BlockSpec((None, CHUNK, kw), prev),
            pl.BlockSpec((None, tm, kw), tile),
            pl.BlockSpec((None, CHUNK, kw), nxt),
            pl.BlockSpec((None, CHUNK, kw), prev),
            pl.BlockSpec((None, tm, kw), tile),
            pl.BlockSpec((None, CHUNK, kw), nxt),
            _const_spec((1, D_MODEL)),
            _const_spec((D_MODEL, 2 * D_MODEL)),
            _const_spec((1, D_MODEL)),
            _const_spec((1, N_HEADS)),
            _const_spec((3, CHUNK, 3 * CHUNK)),
            _const_spec((D_MODEL, D_MODEL)),
            _const_spec((D_MODEL, D_MODEL)),
        ],
        out_specs=pl.BlockSpec((None, tm, D_MODEL), tile),
        out_shape=jax.ShapeDtypeStruct((b, s, D_MODEL), jnp.float32),
        scratch_shapes=[pltpu.VMEM((tm, D_MODEL), jnp.bfloat16)],
        compiler_params=pltpu.CompilerParams(
            dimension_semantics=("arbitrary", "arbitrary"), vmem_limit_bytes=VMEM_LIMIT_BYTES),
        name="attn_mix",
    )(x3, ya3, kk3, kk3, kk3, vv3, vv3, vv3, ln_g, w_b, gq, sink, dist, wb, wo)


def _ffn_kernel(x_ref, ln_ref, w1_ref, w2_ref, o_ref):
    x = x_ref[...]
    h = _rmsnorm_rows(x, ln_ref[...]).astype(jnp.bfloat16)
    acc = x
    for c in range(D_FF // FF_CHUNK):
        a = jnp.maximum(_dot(h, w1_ref[:, c * FF_CHUNK:(c + 1) * FF_CHUNK]), 0.0)
        acc = acc + _dot((a * a).astype(jnp.bfloat16), w2_ref[c * FF_CHUNK:(c + 1) * FF_CHUNK, :])
    o_ref[...] = acc


def _ffn(x2, ln_g, w1, w2):
    t = x2.shape[0]
    tm = TOKENS_FFN
    return pl.pallas_call(
        _ffn_kernel,
        grid=(t // tm,),
        in_specs=[
            pl.BlockSpec((tm, D_MODEL), lambda i: (i, 0)),
            _const_spec((1, D_MODEL)),
            _const_spec((D_MODEL, D_FF)),
            _const_spec((D_FF, D_MODEL)),
        ],
        out_specs=pl.BlockSpec((tm, D_MODEL), lambda i: (i, 0)),
        out_shape=jax.ShapeDtypeStruct((t, D_MODEL), jnp.float32),
        compiler_params=pltpu.CompilerParams(
            dimension_semantics=("arbitrary",), vmem_limit_bytes=VMEM_LIMIT_BYTES),
        name="ffn",
    )(x2, ln_g, w1, w2)


def _distance_tables():
    qi = np.arange(CHUNK)[:, None]
    kj = np.arange(3 * CHUNK)[None, :]
    rel = np.abs(kj - CHUNK - qi)
    in_window = rel <= WINDOW
    tables = []
    for valid in (in_window, in_window & (kj >= CHUNK), in_window & (kj < 2 * CHUNK)):
        tables.append(np.where(valid, rel, MASKED_DISTANCE))
    return jnp.asarray(np.stack(tables), jnp.float32)


def _dup_heads(w):
    lead = w.shape[:-1]
    w = w.reshape(*lead, N_KV, 1, HEAD_DIM)
    return jnp.broadcast_to(w, (*lead, N_KV, 2, HEAD_DIM)).reshape(*lead, N_KV * LANES)


def kernel(x, ln1_g, w_in, a_norm_g, a_w_s, a_b_s, b_q_norm_g, b_k_norm_g, b_sink,
           w_branch_a, w_branch_b, w_out, ln2_g, w_ff1, w_ff2):
    b, s, d = x.shape
    depth = w_in.shape[0]
    assert d == D_MODEL and s % TOKENS_MIX == 0 and (b * s) % TOKENS_KV == 0
    bf = jnp.bfloat16
    o_ga, o_gb, o_za, o_q, o_k, o_v = 0, D_MODEL, 2 * D_MODEL, 4 * D_MODEL, 5 * D_MODEL, 5 * D_MODEL + N_KV * HEAD_DIM

    w_a = jnp.concatenate([w_in[:, :, o_ga:o_gb], w_in[:, :, o_za:o_q]], axis=-1).astype(bf)
    w_b = jnp.concatenate([w_in[:, :, o_gb:o_za], w_in[:, :, o_q:o_k]], axis=-1).astype(bf)
    w_kv = jnp.concatenate([_dup_heads(w_in[:, :, o_k:o_v]), _dup_heads(w_in[:, :, o_v:])], axis=-1).astype(bf)
    gk = jnp.tile(b_k_norm_g, (1, 2 * N_KV))[:, None, :]
    gq = (jnp.tile(b_q_norm_g, (1, N_HEADS)) * (HEAD_DIM ** -0.5))[:, None, :]
    ws = a_w_s.astype(bf)
    bs = jnp.broadcast_to(a_b_s[:, :, :, None], (depth, A_GROUPS, CHUNK, LANES))
    wa, wb, wo = w_branch_a.astype(bf), w_branch_b.astype(bf), w_out.astype(bf)
    w1, w2 = w_ff1.astype(bf), w_ff2.astype(bf)
    dist = _distance_tables()

    x2 = x.reshape(b * s, d)
    for l in range(depth):
        kk, vv = _kv_proj(x2, ln1_g[l][None], w_kv[l], gk[l])
        ya = _gmlp(x2, ln1_g[l][None], w_a[l], a_norm_g[l][None], ws[l], bs[l], wa[l])
        x3 = _attn_mix(x2.reshape(b, s, d), ya.reshape(b, s, d),
                       kk.reshape(b, s, -1), vv.reshape(b, s, -1),
                       ln1_g[l][None], w_b[l], gq[l], b_sink[l][None], dist, wb[l], wo[l])
        x2 = _ffn(x3.reshape(b * s, d), ln2_g[l][None], w1[l], w2[l])
    return x2.reshape(b, s, d)
```

```python
import functools

import jax
import jax.numpy as jnp
import numpy as np
from jax import lax
from jax.experimental import pallas as pl
from jax.experimental.pallas import tpu as pltpu

D_MODEL = 1024
CHUNK = 128
A_GROUPS = 8
HEAD_DIM = 64
N_HEADS = 16
N_KV = 4
GQ = N_HEADS // N_KV
WINDOW = 128
D_FF = 4 * D_MODEL
EPS = 1e-6
LANES = 128
MASKED_LOGIT = -1e30
LOG2E = float(np.log2(np.e))

VMEM_LIMIT_BYTES = 56 * 1024 * 1024

TOKENS_KV = 1024
TOKENS_MIX = 512
TOKENS_FFN = 512
FF_CHUNK = 1024

_SLOPES = [float(2.0 ** (-8.0 * (h + 1.0) / N_HEADS)) for h in range(N_HEADS)]


def _const_spec(shape):
    nd = len(shape)
    return pl.BlockSpec(shape, lambda *_: (0,) * nd, pipeline_mode=pl.Buffered(1))


def _rmsnorm_rows(x, g):
    ms = jnp.mean(x * x, axis=-1, keepdims=True)
    return x * lax.rsqrt(ms + EPS) * g


def _gelu_exact(x):
    return 0.5 * x * (1.0 + lax.erf(x * np.float32(1.0 / np.sqrt(2.0))))


def _dot(a, b):
    return jnp.dot(a, b, preferred_element_type=jnp.float32)


def _kv_proj_kernel(x_ref, ln_ref, w_ref, gk_ref, k_ref, v_ref):
    h = _rmsnorm_rows(x_ref[...], ln_ref[...]).astype(jnp.bfloat16)
    kv = _dot(h, w_ref[...])
    kw = N_KV * LANES
    parts = []
    for g in range(N_KV):
        kg = kv[:, g * LANES:(g + 1) * LANES]
        parts.append(_rmsnorm_rows(kg, gk_ref[:, g * LANES:(g + 1) * LANES]))
    k_ref[...] = jnp.concatenate(parts, axis=-1).astype(jnp.bfloat16)
    v_ref[...] = kv[:, kw:].astype(jnp.bfloat16)


def _kv_proj(x2, ln_g, w_kv, gk):
    t = x2.shape[0]
    tm = min(TOKENS_KV, t)
    kw = N_KV * LANES
    return pl.pallas_call(
        _kv_proj_kernel,
        grid=(t // tm,),
        in_specs=[
            pl.BlockSpec((tm, D_MODEL), lambda i: (i, 0)),
            _const_spec((1, D_MODEL)),
            _const_spec((D_MODEL, 2 * kw)),
            _const_spec((1, kw)),
        ],
        out_specs=[
            pl.BlockSpec((tm, kw), lambda i: (i, 0)),
            pl.BlockSpec((tm, kw), lambda i: (i, 0)),
        ],
        out_shape=[
            jax.ShapeDtypeStruct((t, kw), jnp.bfloat16),
            jax.ShapeDtypeStruct((t, kw), jnp.bfloat16),
        ],
        compiler_params=pltpu.CompilerParams(
            dimension_semantics=("arbitrary",), vmem_limit_bytes=VMEM_LIMIT_BYTES),
        name="kv_proj",
    )(x2, ln_g, w_kv, gk)


def _gmlp_kernel(x_ref, ln_ref, w_ref, gv_ref, ws_ref, bs_ref, wa_ref, o_ref, gated_ref):
    rows = x_ref.shape[0]
    h = _rmsnorm_rows(x_ref[...], ln_ref[...]).astype(jnp.bfloat16)
    u = _gelu_exact(_dot(h, w_ref[:, D_MODEL:2 * D_MODEL]))
    v = _gelu_exact(_dot(h, w_ref[:, 2 * D_MODEL:]))
    vn = _rmsnorm_rows(v, gv_ref[...]).astype(jnp.bfloat16)
    for n in range(rows // CHUNK):
        r0 = n * CHUNK
        for g in range(A_GROUPS):
            c0 = g * LANES
            s = _dot(ws_ref[g], vn[r0:r0 + CHUNK, c0:c0 + LANES]) + bs_ref[g]
            gated_ref[r0:r0 + CHUNK, c0:c0 + LANES] = (
                u[r0:r0 + CHUNK, c0:c0 + LANES] * s).astype(jnp.bfloat16)
    ya = _dot(gated_ref[...], wa_ref[...])
    ga = _dot(h, w_ref[:, :D_MODEL])
    o_ref[...] = (jax.nn.sigmoid(ga) * ya).astype(jnp.bfloat16)


def _gmlp(x2, ln_g, w_a, gv, ws, bs, wa):
    t = x2.shape[0]
    tm = TOKENS_MIX
    return pl.pallas_call(
        _gmlp_kernel,
        grid=(t // tm,),
        in_specs=[
            pl.BlockSpec((tm, D_MODEL), lambda i: (i, 0)),
            _const_spec((1, D_MODEL)),
            _const_spec((D_MODEL, 3 * D_MODEL)),
            _const_spec((1, D_MODEL)),
            _const_spec((A_GROUPS, CHUNK, CHUNK)),
            _const_spec((A_GROUPS, CHUNK, LANES)),
            _const_spec((D_MODEL, D_MODEL)),
        ],
        out_specs=pl.BlockSpec((tm, D_MODEL), lambda i: (i, 0)),
        out_shape=jax.ShapeDtypeStruct((t, D_MODEL), jnp.bfloat16),
        scratch_shapes=[pltpu.VMEM((tm, D_MODEL), jnp.bfloat16)],
        compiler_params=pltpu.CompilerParams(
            dimension_semantics=("arbitrary",), vmem_limit_bytes=VMEM_LIMIT_BYTES),
        name="gmlp",
    )(x2, ln_g, w_a, gv, ws, bs, wa)


def _attn_mix_kernel(x_ref, ya_ref, kp_ref, kc_ref, kn_ref, vp_ref, vc_ref, vn_ref,
                     ln_ref, w_ref, gq_ref, sink_ref, bias_ref, wb_ref, wo_ref,
                     o_ref, attn_ref):
    rows = x_ref.shape[0]
    nblk = rows // CHUNK
    i = pl.program_id(1)
    last = pl.num_programs(1) - 1

    x = x_ref[...]
    h = _rmsnorm_rows(x, ln_ref[...]).astype(jnp.bfloat16)
    q = _dot(h, w_ref[:, D_MODEL:])

    lane = lax.broadcasted_iota(jnp.int32, (rows, LANES), 1)
    low = lane < HEAD_DIM
    q_heads = []
    for pp in range(N_HEADS // 2):
        t = q[:, pp * LANES:(pp + 1) * LANES]
        t2 = t * t
        ssq_all = jnp.sum(t2, axis=-1, keepdims=True)
        ssq_low = jnp.sum(jnp.where(low, t2, 0.0), axis=-1, keepdims=True)
        ssq = jnp.where(low, ssq_low, ssq_all - ssq_low)
        qn = t * lax.rsqrt(ssq * (1.0 / HEAD_DIM) + EPS) * gq_ref[:, pp * LANES:(pp + 1) * LANES]
        q_heads.append(jnp.where(low, qn, 0.0).astype(jnp.bfloat16))
        q_heads.append(jnp.where(low, 0.0, qn).astype(jnp.bfloat16))

    k_ext = jnp.concatenate([kp_ref[...], kc_ref[...], kn_ref[...]], axis=0)
    v_ext = jnp.concatenate([vp_ref[...], vc_ref[...], vn_ref[...]], axis=0)
    ones = jnp.ones((3 * CHUNK, LANES), jnp.bfloat16)
    low_blk = lax.broadcasted_iota(jnp.int32, (CHUNK, LANES), 1) < HEAD_DIM
    pen_first = jnp.where(i == 0, MASKED_LOGIT, 0.0).astype(jnp.float32)
    pen_last = jnp.where(i == last, MASKED_LOGIT, 0.0).astype(jnp.float32)

    for jj in range(nblk):
        r0 = jj * CHUNK
        for g in range(N_KV):
            lhs = jnp.concatenate(
                [q_heads[GQ * g + hh][r0:r0 + CHUNK] for hh in range(GQ)], axis=0)
            kd = k_ext[r0:r0 + 3 * CHUNK, g * LANES:(g + 1) * LANES]
            s = lax.dot_general(lhs, kd, (((1,), (1,)), ((), ())),
                                preferred_element_type=jnp.float32)
            ps, sink_terms = [], []
            for hh in range(GQ):
                head = GQ * g + hh
                sb = []
                for c in range(3):
                    t = (s[hh * CHUNK:(hh + 1) * CHUNK, c * LANES:(c + 1) * LANES]
                         + bias_ref[head, :, c * LANES:(c + 1) * LANES])
                    if c == 0 and jj == 0:
                        t = t + pen_first
                    if c == 2 and jj == nblk - 1:
                        t = t + pen_last
                    sb.append(t)
                row_max = jnp.max(jnp.maximum(jnp.maximum(sb[0], sb[1]), sb[2]), axis=-1, keepdims=True)
                sink = sink_ref[head]
                m = jnp.maximum(row_max, sink)
                ps.append(jnp.concatenate([jnp.exp2(t - m).astype(jnp.bfloat16) for t in sb], axis=1))
                sink_terms.append(jnp.exp2(sink - m))
            p = jnp.concatenate(ps, axis=0)
            vd = jnp.concatenate(
                [v_ext[r0:r0 + 3 * CHUNK, g * LANES:(g + 1) * LANES], ones], axis=1)
            r = _dot(p, vd)
            outs = []
            for hh in range(GQ):
                rh = r[hh * CHUNK:(hh + 1) * CHUNK]
                outs.append(rh[:, :LANES] / (rh[:, LANES:] + sink_terms[hh]))
            for pr in range(GQ // 2):
                pair = jnp.where(low_blk, outs[2 * pr], outs[2 * pr + 1])
                c0 = (g * (GQ // 2) + pr) * LANES
                attn_ref[r0:r0 + CHUNK, c0:c0 + LANES] = pair.astype(jnp.bfloat16)

    yb = _dot(attn_ref[...], wb_ref[...])
    gb = _dot(h, w_ref[:, :D_MODEL])
    mixed = (jax.nn.sigmoid(gb) * yb + ya_ref[...].astype(jnp.float32)).astype(jnp.bfloat16)
    o_ref[...] = x + _dot(mixed, wo_ref[...])


def _attn_mix(x3, ya3, kk3, vv3, ln_g, w_b, gq, sink, bias, wb, wo):
    b, s, _ = x3.shape
    tm = TOKENS_MIX
    nblk = tm // CHUNK
    nb = s // CHUNK
    kw = N_KV * LANES
    tile = lambda bi, i: (bi, i, 0)
    prev = lambda bi, i: (bi, jnp.maximum(i * nblk - 1, 0), 0)
    nxt = lambda bi, i: (bi, jnp.minimum((i + 1) * nblk, nb - 1), 0)
    return pl.pallas_call(
        _attn_mix_kernel,
        grid=(b, s // tm),
        in_specs=[
            pl.BlockSpec((None, tm, D_MODEL), tile),
            pl.BlockSpec((None, tm, D_MODEL), tile),
            pl.BlockSpec((None, CHUNK, kw), prev),
            pl.BlockSpec((None, tm, kw), tile),
            pl.BlockSpec((None, CHUNK, kw), nxt),
            pl.BlockSpec((None, CHUNK, kw), prev),
            pl.BlockSpec((None, tm, kw), tile),
            pl.BlockSpec((None, CHUNK, kw), nxt),
            _const_spec((1, D_MODEL)),
            _const_spec((D_MODEL, 2 * D_MODEL)),
            _const_spec((1, D_MODEL)),
            _const_spec((N_HEADS, 1, LANES)),
            _const_spec((N_HEADS, CHUNK, 3 * CHUNK)),
            _const_spec((D_MODEL, D_MODEL)),
            _const_spec((D_MODEL, D_MODEL)),
        ],
        out_specs=pl.BlockSpec((None, tm, D_MODEL), tile),
        out_shape=jax.ShapeDtypeStruct((b, s, D_MODEL), jnp.float32),
        scratch_shapes=[pltpu.VMEM((tm, D_MODEL), jnp.bfloat16)],
        compiler_params=pltpu.CompilerParams(
            dimension_semantics=("arbitrary", "arbitrary"), vmem_limit_bytes=VMEM_LIMIT_BYTES),
        name="attn_mix",
    )(x3, ya3, kk3, kk3, kk3, vv3, vv3, vv3, ln_g, w_b, gq, sink, bias, wb, wo)


def _ffn_kernel(x_ref, ln_ref, w1_ref, w2_ref, o_ref):
    x = x_ref[...]
    h = _rmsnorm_rows(x, ln_ref[...]).astype(jnp.bfloat16)
    acc = x
    for c in range(D_FF // FF_CHUNK):
        a = jnp.maximum(_dot(h, w1_ref[:, c * FF_CHUNK:(c + 1) * FF_CHUNK]), 0.0)
        acc = acc + _dot((a * a).astype(jnp.bfloat16), w2_ref[c * FF_CHUNK:(c + 1) * FF_CHUNK, :])
    o_ref[...] = acc


def _ffn(x2, ln_g, w1, w2):
    t = x2.shape[0]
    tm = TOKENS_FFN
    return pl.pallas_call(
        _ffn_kernel,
        grid=(t // tm,),
        in_specs=[
            pl.BlockSpec((tm, D_MODEL), lambda i: (i, 0)),
            _const_spec((1, D_MODEL)),
            _const_spec((D_MODEL, D_FF)),
            _const_spec((D_FF, D_MODEL)),
        ],
        out_specs=pl.BlockSpec((tm, D_MODEL), lambda i: (i, 0)),
        out_shape=jax.ShapeDtypeStruct((t, D_MODEL), jnp.float32),
        compiler_params=pltpu.CompilerParams(
            dimension_semantics=("arbitrary",), vmem_limit_bytes=VMEM_LIMIT_BYTES),
        name="ffn",
    )(x2, ln_g, w1, w2)


def _alibi_bias():
    qi = np.arange(CHUNK)[:, None]
    kj = np.arange(3 * CHUNK)[None, :]
    rel = np.abs(kj - CHUNK - qi).astype(np.float64)
    slopes = np.asarray(_SLOPES, np.float64)[:, None, None] * LOG2E
    bias = np.where(rel <= WINDOW, -slopes * rel, MASKED_LOGIT)
    return jnp.asarray(bias, jnp.float32)


def _dup_heads(w):
    lead = w.shape[:-1]
    w = w.reshape(*lead, N_KV, 1, HEAD_DIM)
    return jnp.broadcast_to(w, (*lead, N_KV, 2, HEAD_DIM)).reshape(*lead, N_KV * LANES)


def kernel(x, ln1_g, w_in, a_norm_g, a_w_s, a_b_s, b_q_norm_g, b_k_norm_g, b_sink,
           w_branch_a, w_branch_b, w_out, ln2_g, w_ff1, w_ff2):
    b, s, d = x.shape
    depth = w_in.shape[0]
    assert d == D_MODEL and s % TOKENS_MIX == 0 and (b * s) % TOKENS_KV == 0
    bf = jnp.bfloat16
    o_ga, o_gb, o_za, o_q, o_k, o_v = 0, D_MODEL, 2 * D_MODEL, 4 * D_MODEL, 5 * D_MODEL, 5 * D_MODEL + N_KV * HEAD_DIM

    w_a = jnp.concatenate([w_in[:, :, o_ga:o_gb], w_in[:, :, o_za:o_q]], axis=-1).astype(bf)
    w_b = jnp.concatenate([w_in[:, :, o_gb:o_za], w_in[:, :, o_q:o_k]], axis=-1).astype(bf)
    w_kv = jnp.concatenate([_dup_heads(w_in[:, :, o_k:o_v]), _dup_heads(w_in[:, :, o_v:])], axis=-1).astype(bf)
    gk = jnp.tile(b_k_norm_g, (1, 2 * N_KV))[:, None, :]
    gq = (jnp.tile(b_q_norm_g, (1, N_HEADS)) * (HEAD_DIM ** -0.5 * LOG2E))[:, None, :]
    sink = jnp.broadcast_to((b_sink * LOG2E)[:, :, None, None], (depth, N_HEADS, 1, LANES))
    ws = a_w_s.astype(bf)
    bs = jnp.broadcast_to(a_b_s[:, :, :, None], (depth, A_GROUPS, CHUNK, LANES))
    wa, wb, wo = w_branch_a.astype(bf), w_branch_b.astype(bf), w_out.astype(bf)
    w1, w2 = w_ff1.astype(bf), w_ff2.astype(bf)
    bias = _alibi_bias()

    x2 = x.reshape(b * s, d)
    for l in range(depth):
        kk, vv = _kv_proj(x2, ln1_g[l][None], w_kv[l], gk[l])
        ya = _gmlp(x2, ln1_g[l][None], w_a[l], a_norm_g[l][None], ws[l], bs[l], wa[l])
        x3 = _attn_mix(x2.reshape(b, s, d), ya.reshape(b, s, d),
                       kk.reshape(b, s, -1), vv.reshape(b, s, -1),
                       ln1_g[l][None], w_b[l], gq[l], sink[l], bias, wb[l], wo[l])
        x2 = _ffn(x3.reshape(b * s, d), ln2_g[l][None], w1[l], w2[l])
    return x2.reshape(b, s, d)
```

```python
import functools

import jax
import jax.numpy as jnp
import numpy as np
from jax import lax
from jax.experimental import pallas as pl
from jax.experimental.pallas import tpu as pltpu

D_MODEL = 1024
CHUNK = 128
A_GROUPS = 8
HEAD_DIM = 64
N_HEADS = 16
N_KV = 4
GQ = N_HEADS // N_KV
WINDOW = 128
D_FF = 4 * D_MODEL
EPS = 1e-6
LANES = 128
MASKED_LOGIT = -1e30
LOG2E = float(np.log2(np.e))

VMEM_LIMIT_BYTES = 56 * 1024 * 1024

TOKENS_KV = 1024
TOKENS_MIX = 512
TOKENS_FFN = 512
FF_CHUNK = 1024

_SLOPES = [float(2.0 ** (-8.0 * (h + 1.0) / N_HEADS)) for h in range(N_HEADS)]

_PAIR_HEADS = [(2 * GQ * (pp // GQ) + pp % GQ, 2 * GQ * (pp // GQ) + GQ + pp % GQ)
               for pp in range(N_HEADS // 2)]
_HEAD_ORDER = [h for pair in _PAIR_HEADS for h in pair]


def _const_spec(shape):
    nd = len(shape)
    return pl.BlockSpec(shape, lambda *_: (0,) * nd, pipeline_mode=pl.Buffered(1))


def _rmsnorm_rows(x, g):
    ms = jnp.mean(x * x, axis=-1, keepdims=True)
    return x * lax.rsqrt(ms + EPS) * g


def _gelu_exact(x):
    return 0.5 * x * (1.0 + lax.erf(x * np.float32(1.0 / np.sqrt(2.0))))


def _dot(a, b):
    return jnp.dot(a, b, preferred_element_type=jnp.float32)


def _head_pair_sumsq(t, low):
    t2 = t * t
    ssq_all = jnp.sum(t2, axis=-1, keepdims=True)
    ssq_low = jnp.sum(jnp.where(low, t2, 0.0), axis=-1, keepdims=True)
    return jnp.where(low, ssq_low, ssq_all - ssq_low)


def _kv_proj_kernel(x_ref, ln_ref, w_ref, gk_ref, k_ref, v_ref):
    rows = x_ref.shape[0]
    h = _rmsnorm_rows(x_ref[...], ln_ref[...]).astype(jnp.bfloat16)
    kv = _dot(h, w_ref[...])
    kw = N_KV * HEAD_DIM
    low = lax.broadcasted_iota(jnp.int32, (rows, LANES), 1) < HEAD_DIM
    parts = []
    for pp in range(kw // LANES):
        t = kv[:, pp * LANES:(pp + 1) * LANES]
        ssq = _head_pair_sumsq(t, low)
        parts.append(t * lax.rsqrt(ssq * (1.0 / HEAD_DIM) + EPS) * gk_ref[:, pp * LANES:(pp + 1) * LANES])
    k_ref[...] = jnp.concatenate(parts, axis=-1).astype(jnp.bfloat16)
    v_ref[...] = kv[:, kw:].astype(jnp.bfloat16)


def _kv_proj(x2, ln_g, w_kv, gk):
    t = x2.shape[0]
    tm = min(TOKENS_KV, t)
    kw = N_KV * HEAD_DIM
    return pl.pallas_call(
        _kv_proj_kernel,
        grid=(t // tm,),
        in_specs=[
            pl.BlockSpec((tm, D_MODEL), lambda i: (i, 0)),
            _const_spec((1, D_MODEL)),
            _const_spec((D_MODEL, 2 * kw)),
            _const_spec((1, kw)),
        ],
        out_specs=[
            pl.BlockSpec((tm, kw), lambda i: (i, 0)),
            pl.BlockSpec((tm, kw), lambda i: (i, 0)),
        ],
        out_shape=[
            jax.ShapeDtypeStruct((t, kw), jnp.bfloat16),
            jax.ShapeDtypeStruct((t, kw), jnp.bfloat16),
        ],
        compiler_params=pltpu.CompilerParams(
            dimension_semantics=("arbitrary",), vmem_limit_bytes=VMEM_LIMIT_BYTES),
        name="kv_proj",
    )(x2, ln_g, w_kv, gk)


def _gmlp_kernel(x_ref, ln_ref, w_ref, gv_ref, ws_ref, bs_ref, wa_ref, o_ref, gated_ref):
    rows = x_ref.shape[0]
    h = _rmsnorm_rows(x_ref[...], ln_ref[...]).astype(jnp.bfloat16)
    u = _gelu_exact(_dot(h, w_ref[:, D_MODEL:2 * D_MODEL]))
    v = _gelu_exact(_dot(h, w_ref[:, 2 * D_MODEL:]))
    vn = _rmsnorm_rows(v, gv_ref[...]).astype(jnp.bfloat16)
    zeros = jnp.zeros((CHUNK, LANES), jnp.bfloat16)
    for n in range(rows // CHUNK):
        r0 = n * CHUNK
        for gp in range(A_GROUPS // 2):
            c0 = 2 * gp * LANES
            v0 = vn[r0:r0 + CHUNK, c0:c0 + LANES]
            v1 = vn[r0:r0 + CHUNK, c0 + LANES:c0 + 2 * LANES]
            rhs = jnp.concatenate([jnp.concatenate([v0, zeros], axis=1),
                                   jnp.concatenate([zeros, v1], axis=1)], axis=0)
            s = _dot(ws_ref[gp], rhs) + bs_ref[gp]
            gated_ref[r0:r0 + CHUNK, c0:c0 + 2 * LANES] = (
                u[r0:r0 + CHUNK, c0:c0 + 2 * LANES] * s).astype(jnp.bfloat16)
    ya = _dot(gated_ref[...], wa_ref[...])
    ga = _dot(h, w_ref[:, :D_MODEL])
    o_ref[...] = (jax.nn.sigmoid(ga) * ya).astype(jnp.bfloat16)


def _gmlp(x2, ln_g, w_a, gv, ws, bs, wa):
    t = x2.shape[0]
    tm = TOKENS_MIX
    return pl.pallas_call(
        _gmlp_kernel,
        grid=(t // tm,),
        in_specs=[
            pl.BlockSpec((tm, D_MODEL), lambda i: (i, 0)),
            _const_spec((1, D_MODEL)),
            _const_spec((D_MODEL, 3 * D_MODEL)),
            _const_spec((1, D_MODEL)),
            _const_spec((A_GROUPS // 2, CHUNK, 2 * CHUNK)),
            _const_spec((A_GROUPS // 2, CHUNK, 2 * LANES)),
            _const_spec((D_MODEL, D_MODEL)),
        ],
        out_specs=pl.BlockSpec((tm, D_MODEL), lambda i: (i, 0)),
        out_shape=jax.ShapeDtypeStruct((t, D_MODEL), jnp.bfloat16),
        scratch_shapes=[pltpu.VMEM((tm, D_MODEL), jnp.bfloat16)],
        compiler_params=pltpu.CompilerParams(
            dimension_semantics=("arbitrary",), vmem_limit_bytes=VMEM_LIMIT_BYTES),
        name="gmlp",
    )(x2, ln_g, w_a, gv, ws, bs, wa)


def _attn_mix_kernel(x_ref, ya_ref, kp_ref, kc_ref, kn_ref, vp_ref, vc_ref, vn_ref,
                     ln_ref, w_ref, gq_ref, sink_ref, bias_ref, wb_ref, wo_ref,
                     o_ref, attn_ref):
    rows = x_ref.shape[0]
    nblk = rows // CHUNK
    i = pl.program_id(1)
    last = pl.num_programs(1) - 1

    x = x_ref[...]
    h = _rmsnorm_rows(x, ln_ref[...]).astype(jnp.bfloat16)
    q = _dot(h, w_ref[:, D_MODEL:])

    low = lax.broadcasted_iota(jnp.int32, (rows, LANES), 1) < HEAD_DIM
    q_low, q_high = [], []
    for pp in range(N_HEADS // 2):
        t = q[:, pp * LANES:(pp + 1) * LANES]
        ssq = _head_pair_sumsq(t, low)
        qn = t * lax.rsqrt(ssq * (1.0 / HEAD_DIM) + EPS) * gq_ref[:, pp * LANES:(pp + 1) * LANES]
        q_low.append(jnp.where(low, qn, 0.0).astype(jnp.bfloat16))
        q_high.append(jnp.where(low, 0.0, qn).astype(jnp.bfloat16))

    k_ext = jnp.concatenate([kp_ref[...], kc_ref[...], kn_ref[...]], axis=0)
    v_ext = jnp.concatenate([vp_ref[...], vc_ref[...], vn_ref[...]], axis=0)
    ones = jnp.ones((3 * CHUNK, LANES), jnp.bfloat16)
    low_blk = lax.broadcasted_iota(jnp.int32, (CHUNK, LANES), 1) < HEAD_DIM
    pen_first = jnp.where(i == 0, MASKED_LOGIT, 0.0).astype(jnp.float32)
    pen_last = jnp.where(i == last, MASKED_LOGIT, 0.0).astype(jnp.float32)

    def scores(jj, pp0):
        r0 = jj * CHUNK
        kv_tile = pp0 // (N_HEADS // N_KV)
        lhs = jnp.concatenate(
            [qh[pp][r0:r0 + CHUNK] for pp in (pp0, pp0 + 1) for qh in (q_low, q_high)], axis=0)
        kd = k_ext[r0:r0 + 3 * CHUNK, kv_tile * LANES:(kv_tile + 1) * LANES]
        return lax.dot_general(lhs, kd, (((1,), (1,)), ((), ())),
                               preferred_element_type=jnp.float32)

    def softmax_numerators(jj, pp0, s):
        ps, sink_terms = [], []
        for hh, head in enumerate(_PAIR_HEADS[pp0] + _PAIR_HEADS[pp0 + 1]):
            sb = []
            for c in range(3):
                t = (s[hh * CHUNK:(hh + 1) * CHUNK, c * LANES:(c + 1) * LANES]
                     + bias_ref[head, :, c * LANES:(c + 1) * LANES])
                if c == 0 and jj == 0:
                    t = t + pen_first
                if c == 2 and jj == nblk - 1:
                    t = t + pen_last
                sb.append(t)
            row_max = jnp.max(jnp.maximum(jnp.maximum(sb[0], sb[1]), sb[2]), axis=-1, keepdims=True)
            sink = sink_ref[head]
            m = jnp.maximum(row_max, sink)
            ps.append(jnp.concatenate([jnp.exp2(t - m).astype(jnp.bfloat16) for t in sb], axis=1))
            sink_terms.append(jnp.exp2(sink - m))
        return jnp.concatenate(ps, axis=0), sink_terms

    def weighted_values(jj, pp0, p, sink_terms):
        r0 = jj * CHUNK
        kv_tile = pp0 // (N_HEADS // N_KV)
        vd = jnp.concatenate(
            [v_ext[r0:r0 + 3 * CHUNK, kv_tile * LANES:(kv_tile + 1) * LANES], ones], axis=1)
        r = _dot(p, vd)
        outs = []
        for hh in range(4):
            rh = r[hh * CHUNK:(hh + 1) * CHUNK]
            outs.append(rh[:, :LANES] / (rh[:, LANES:] + sink_terms[hh]))
        for pr in range(2):
            pair = jnp.where(low_blk, outs[2 * pr], outs[2 * pr + 1])
            c0 = (pp0 + pr) * LANES
            attn_ref[r0:r0 + CHUNK, c0:c0 + LANES] = pair.astype(jnp.bfloat16)

    steps = [(jj, pp0) for jj in range(nblk) for pp0 in range(0, N_HEADS // 2, 2)]
    s_next = scores(*steps[0])
    for n, step in enumerate(steps):
        s_cur = s_next
        if n + 1 < len(steps):
            s_next = scores(*steps[n + 1])
        p, sink_terms = softmax_numerators(*step, s_cur)
        weighted_values(*step, p, sink_terms)

    yb = _dot(attn_ref[...], wb_ref[...])
    gb = _dot(h, w_ref[:, :D_MODEL])
    mixed = (jax.nn.sigmoid(gb) * yb + ya_ref[...].astype(jnp.float32)).astype(jnp.bfloat16)
    o_ref[...] = x + _dot(mixed, wo_ref[...])


def _attn_mix(x3, ya3, kk3, vv3, ln_g, w_b, gq, sink, bias, wb, wo):
    b, s, _ = x3.shape
    tm = TOKENS_MIX
    nblk = tm // CHUNK
    nb = s // CHUNK
    kw = N_KV * HEAD_DIM
    tile = lambda bi, i: (bi, i, 0)
    prev = lambda bi, i: (bi, jnp.maximum(i * nblk - 1, 0), 0)
    nxt = lambda bi, i: (bi, jnp.minimum((i + 1) * nblk, nb - 1), 0)
    return pl.pallas_call(
        _attn_mix_kernel,
        grid=(b, s // tm),
        in_specs=[
            pl.BlockSpec((None, tm, D_MODEL), tile),
            pl.BlockSpec((None, tm, D_MODEL), tile),
            pl.BlockSpec((None, CHUNK, kw), prev),
            pl.BlockSpec((None, tm, kw), tile),
            pl.BlockSpec((None, CHUNK, kw), nxt),
            pl.BlockSpec((None, CHUNK, kw), prev),
            pl.BlockSpec((None, tm, kw), tile),
            pl.BlockSpec((None, CHUNK, kw), nxt),
            _const_spec((1, D_MODEL)),
            _const_spec((D_MODEL, 2 * D_MODEL)),
            _const_spec((1, D_MODEL)),
            _const_spec((N_HEADS, 1, LANES)),
            _const_spec((N_HEADS, CHUNK, 3 * CHUNK)),
            _const_spec((D_MODEL, D_MODEL)),
            _const_spec((D_MODEL, D_MODEL)),
        ],
        out_specs=pl.BlockSpec((None, tm, D_MODEL), tile),
        out_shape=jax.ShapeDtypeStruct((b, s, D_MODEL), jnp.float32),
        scratch_shapes=[pltpu.VMEM((tm, D_MODEL), jnp.bfloat16)],
        compiler_params=pltpu.CompilerParams(
            dimension_semantics=("arbitrary", "arbitrary"), vmem_limit_bytes=VMEM_LIMIT_BYTES),
        name="attn_mix",
    )(x3, ya3, kk3, kk3, kk3, vv3, vv3, vv3, ln_g, w_b, gq, sink, bias, wb, wo)


def _ffn_kernel(x_ref, ln_ref, w1_ref, w2_ref, o_ref):
    x = x_ref[...]
    h = _rmsnorm_rows(x, ln_ref[...]).astype(jnp.bfloat16)
    acc = x
    for c in range(D_FF // FF_CHUNK):
        a = jnp.maximum(_dot(h, w1_ref[:, c * FF_CHUNK:(c + 1) * FF_CHUNK]), 0.0)
        acc = acc + _dot((a * a).astype(jnp.bfloat16), w2_ref[c * FF_CHUNK:(c + 1) * FF_CHUNK, :])
    o_ref[...] = acc


def _ffn(x2, ln_g, w1, w2):
    t = x2.shape[0]
    tm = TOKENS_FFN
    return pl.pallas_call(
        _ffn_kernel,
        grid=(t // tm,),
        in_specs=[
            pl.BlockSpec((tm, D_MODEL), lambda i: (i, 0)),
            _const_spec((1, D_MODEL)),
            _const_spec((D_MODEL, D_FF)),
            _const_spec((D_FF, D_MODEL)),
        ],
        out_specs=pl.BlockSpec((tm, D_MODEL), lambda i: (i, 0)),
        out_shape=jax.ShapeDtypeStruct((t, D_MODEL), jnp.float32),
        compiler_params=pltpu.CompilerParams(
            dimension_semantics=("arbitrary",), vmem_limit_bytes=VMEM_LIMIT_BYTES),
        name="ffn",
    )(x2, ln_g, w1, w2)


def _alibi_bias():
    qi = np.arange(CHUNK)[:, None]
    kj = np.arange(3 * CHUNK)[None, :]
    rel = np.abs(kj - CHUNK - qi).astype(np.float64)
    slopes = np.asarray(_SLOPES, np.float64)[:, None, None] * LOG2E
    bias = np.where(rel <= WINDOW, -slopes * rel, MASKED_LOGIT)
    return jnp.asarray(bias, jnp.float32)


def kernel(x, ln1_g, w_in, a_norm_g, a_w_s, a_b_s, b_q_norm_g, b_k_norm_g, b_sink,
           w_branch_a, w_branch_b, w_out, ln2_g, w_ff1, w_ff2):
    b, s, d = x.shape
    depth = w_in.shape[0]
    assert d == D_MODEL and s % TOKENS_MIX == 0 and (b * s) % TOKENS_KV == 0
    bf = jnp.bfloat16
    o_ga, o_gb, o_za, o_q, o_k, o_v = 0, D_MODEL, 2 * D_MODEL, 4 * D_MODEL, 5 * D_MODEL, 5 * D_MODEL + N_KV * HEAD_DIM

    w_a = jnp.concatenate([w_in[:, :, o_ga:o_gb], w_in[:, :, o_za:o_q]], axis=-1).astype(bf)
    order = np.asarray(_HEAD_ORDER)
    w_q = w_in[:, :, o_q:o_k].reshape(depth, D_MODEL, N_HEADS, HEAD_DIM)[:, :, order].reshape(depth, D_MODEL, D_MODEL)
    w_b = jnp.concatenate([w_in[:, :, o_gb:o_za], w_q], axis=-1).astype(bf)
    w_kv = w_in[:, :, o_k:].astype(bf)
    gk = jnp.tile(b_k_norm_g, (1, N_KV))[:, None, :]
    gq = (jnp.tile(b_q_norm_g, (1, N_HEADS)) * (HEAD_DIM ** -0.5 * LOG2E))[:, None, :]
    sink = jnp.broadcast_to((b_sink * LOG2E)[:, :, None, None], (depth, N_HEADS, 1, LANES))
    ws = a_w_s.reshape(depth, A_GROUPS // 2, 2, CHUNK, CHUNK).transpose(0, 1, 3, 2, 4)
    ws = ws.reshape(depth, A_GROUPS // 2, CHUNK, 2 * CHUNK).astype(bf)
    bs = jnp.broadcast_to(a_b_s.reshape(depth, A_GROUPS // 2, 2, CHUNK, 1).transpose(0, 1, 3, 2, 4),
                          (depth, A_GROUPS // 2, CHUNK, 2, LANES)).reshape(depth, A_GROUPS // 2, CHUNK, 2 * LANES)
    wa, wo = w_branch_a.astype(bf), w_out.astype(bf)
    wb = w_branch_b.reshape(depth, N_HEADS, HEAD_DIM, D_MODEL)[:, order].reshape(depth, D_MODEL, D_MODEL).astype(bf)
    w1, w2 = w_ff1.astype(bf), w_ff2.astype(bf)
    bias = _alibi_bias()

    x2 = x.reshape(b * s, d)
    for l in range(depth):
        kk, vv = _kv_proj(x2, ln1_g[l][None], w_kv[l], gk[l])
        ya = _gmlp(x2, ln1_g[l][None], w_a[l], a_norm_g[l][None], ws[l], bs[l], wa[l])
        x3 = _attn_mix(x2.reshape(b, s, d), ya.reshape(b, s, d),
                       kk.reshape(b, s, -1), vv.reshape(b, s, -1),
                       ln1_g[l][None], w_b[l], gq[l], sink[l], bias, wb[l], wo[l])
        x2 = _ffn(x3.reshape(b * s, d), ln2_g[l][None], w1[l], w2[l])
    return x2.reshape(b, s, d)
```

```python
import functools

import jax
import jax.numpy as jnp
import numpy as np
from jax import lax
from jax.experimental import pallas as pl
from jax.experimental.pallas import tpu as pltpu

D_MODEL = 1024
CHUNK = 128
A_GROUPS = 8
HEAD_DIM = 64
N_HEADS = 16
N_KV = 4
GQ = N_HEADS // N_KV
WINDOW = 128
D_FF = 4 * D_MODEL
EPS = 1e-6
LANES = 128
MASKED_LOGIT = -1e30
LOG2E = float(np.log2(np.e))

VMEM_LIMIT_BYTES = 56 * 1024 * 1024

TOKENS_GMLP = 1024
TOKENS_MIX = 512
TOKENS_FFN = 512
FF_CHUNK = 1024

_SLOPES = [float(2.0 ** (-8.0 * (h + 1.0) / N_HEADS)) for h in range(N_HEADS)]

_PAIR_HEADS = [(2 * GQ * (pp // GQ) + pp % GQ, 2 * GQ * (pp // GQ) + GQ + pp % GQ)
               for pp in range(N_HEADS // 2)]
_HEAD_ORDER = [h for pair in _PAIR_HEADS for h in pair]


def _const_spec(shape):
    nd = len(shape)
    return pl.BlockSpec(shape, lambda *_: (0,) * nd, pipeline_mode=pl.Buffered(1))


def _rmsnorm_rows(x, g):
    ms = jnp.mean(x * x, axis=-1, keepdims=True)
    return x * lax.rsqrt(ms + EPS) * g


def _gelu_exact(x):
    return 0.5 * x * (1.0 + lax.erf(x * np.float32(1.0 / np.sqrt(2.0))))


def _dot(a, b):
    return jnp.dot(a, b, preferred_element_type=jnp.float32)


def _head_pair_sumsq(t, low):
    t2 = t * t
    ssq_all = jnp.sum(t2, axis=-1, keepdims=True)
    ssq_low = jnp.sum(jnp.where(low, t2, 0.0), axis=-1, keepdims=True)
    return jnp.where(low, ssq_low, ssq_all - ssq_low)


def _gmlp_kv_kernel(x_ref, ln_ref, w_ref, wkv_ref, gk_ref, gv_ref, ws_ref, bs_ref, wa_ref,
                    ya_ref, h_ref, k_ref, v_ref, gated_ref):
    half = x_ref.shape[0] // 2
    kw = N_KV * HEAD_DIM
    low = lax.broadcasted_iota(jnp.int32, (half, LANES), 1) < HEAD_DIM

    def project(r):
        rows = pl.ds(r * half, half)
        h = _rmsnorm_rows(x_ref[rows, :], ln_ref[...]).astype(jnp.bfloat16)
        h_ref[rows, :] = h
        u = _gelu_exact(_dot(h, w_ref[:, D_MODEL:2 * D_MODEL]))
        v = _gelu_exact(_dot(h, w_ref[:, 2 * D_MODEL:]))
        vn = _rmsnorm_rows(v, gv_ref[...]).astype(jnp.bfloat16)
        kv = _dot(h, wkv_ref[...])
        parts = []
        for pp in range(kw // LANES):
            t = kv[:, pp * LANES:(pp + 1) * LANES]
            ssq = _head_pair_sumsq(t, low)
            parts.append(t * lax.rsqrt(ssq * (1.0 / HEAD_DIM) + EPS) * gk_ref[:, pp * LANES:(pp + 1) * LANES])
        k_ref[rows, :] = jnp.concatenate(parts, axis=-1).astype(jnp.bfloat16)
        v_ref[rows, :] = kv[:, kw:].astype(jnp.bfloat16)
        return h, u, vn

    def gate_and_mix(r, h, u, vn):
        base = r * half
        for n in range(half // CHUNK):
            r0 = n * CHUNK
            for g in range(A_GROUPS):
                c0 = g * LANES
                s = _dot(ws_ref[g], vn[r0:r0 + CHUNK, c0:c0 + LANES]) + bs_ref[g]
                gated_ref[base + r0:base + r0 + CHUNK, c0:c0 + LANES] = (
                    u[r0:r0 + CHUNK, c0:c0 + LANES] * s).astype(jnp.bfloat16)
        rows = pl.ds(base, half)
        ya = _dot(gated_ref[rows, :], wa_ref[...])
        ga = _dot(h, w_ref[:, :D_MODEL])
        ya_ref[rows, :] = (jax.nn.sigmoid(ga) * ya).astype(jnp.bfloat16)

    first = project(0)
    second = project(1)
    gate_and_mix(0, *first)
    gate_and_mix(1, *second)


def _gmlp_kv(x2, ln_g, w_a, w_kv, gk, gv, ws, bs, wa):
    t = x2.shape[0]
    tm = TOKENS_GMLP
    kw = N_KV * HEAD_DIM
    rows = lambda width: pl.BlockSpec((tm, width), lambda i: (i, 0))
    return pl.pallas_call(
        _gmlp_kv_kernel,
        grid=(t // tm,),
        in_specs=[
            rows(D_MODEL),
            _const_spec((1, D_MODEL)),
            _const_spec((D_MODEL, 3 * D_MODEL)),
            _const_spec((D_MODEL, 2 * kw)),
            _const_spec((1, kw)),
            _const_spec((1, D_MODEL)),
            _const_spec((A_GROUPS, CHUNK, CHUNK)),
            _const_spec((A_GROUPS, CHUNK, LANES)),
            _const_spec((D_MODEL, D_MODEL)),
        ],
        out_specs=[rows(D_MODEL), rows(D_MODEL), rows(kw), rows(kw)],
        out_shape=[
            jax.ShapeDtypeStruct((t, D_MODEL), jnp.bfloat16),
            jax.ShapeDtypeStruct((t, D_MODEL), jnp.bfloat16),
            jax.ShapeDtypeStruct((t, kw), jnp.bfloat16),
            jax.ShapeDtypeStruct((t, kw), jnp.bfloat16),
        ],
        scratch_shapes=[pltpu.VMEM((tm, D_MODEL), jnp.bfloat16)],
        compiler_params=pltpu.CompilerParams(
            dimension_semantics=("arbitrary",), vmem_limit_bytes=VMEM_LIMIT_BYTES),
        name="gmlp_kv",
    )(x2, ln_g, w_a, w_kv, gk, gv, ws, bs, wa)


def _attn_mix_kernel(x_ref, h_ref, ya_ref, kp_ref, kc_ref, kn_ref, vp_ref, vc_ref, vn_ref,
                     w_ref, gq_ref, sink_ref, bias_ref, wb_ref, wo_ref,
                     o_ref, attn_ref):
    rows = x_ref.shape[0]
    nblk = rows // CHUNK
    i = pl.program_id(1)
    last = pl.num_programs(1) - 1

    h = h_ref[...]
    q = _dot(h, w_ref[:, D_MODEL:])

    low = lax.broadcasted_iota(jnp.int32, (rows, LANES), 1) < HEAD_DIM
    q_low, q_high = [], []
    for pp in range(N_HEADS // 2):
        t = q[:, pp * LANES:(pp + 1) * LANES]
        ssq = _head_pair_sumsq(t, low)
        qn = t * lax.rsqrt(ssq * (1.0 / HEAD_DIM) + EPS) * gq_ref[:, pp * LANES:(pp + 1) * LANES]
        q_low.append(jnp.where(low, qn, 0.0).astype(jnp.bfloat16))
        q_high.append(jnp.where(low, 0.0, qn).astype(jnp.bfloat16))

    k_ext = jnp.concatenate([kp_ref[...], kc_ref[...], kn_ref[...]], axis=0)
    v_ext = jnp.concatenate([vp_ref[...], vc_ref[...], vn_ref[...]], axis=0)
    ones = jnp.ones((3 * CHUNK, LANES), jnp.bfloat16)
    low_blk = lax.broadcasted_iota(jnp.int32, (CHUNK, LANES), 1) < HEAD_DIM
    pen_first = jnp.where(i == 0, MASKED_LOGIT, 0.0).astype(jnp.float32)
    pen_last = jnp.where(i == last, MASKED_LOGIT, 0.0).astype(jnp.float32)

    def scores(jj, pp0):
        r0 = jj * CHUNK
        kv_tile = pp0 // (N_HEADS // N_KV)
        lhs = jnp.concatenate(
            [qh[pp][r0:r0 + CHUNK] for pp in (pp0, pp0 + 1) for qh in (q_low, q_high)], axis=0)
        kd = k_ext[r0:r0 + 3 * CHUNK, kv_tile * LANES:(kv_tile + 1) * LANES]
        return lax.dot_general(lhs, kd, (((1,), (1,)), ((), ())),
                               preferred_element_type=jnp.float32)

    def softmax_numerators(jj, pp0, s):
        ps, sink_terms = [], []
        for hh, head in enumerate(_PAIR_HEADS[pp0] + _PAIR_HEADS[pp0 + 1]):
            sb = []
            for c in range(3):
                t = (s[hh * CHUNK:(hh + 1) * CHUNK, c * LANES:(c + 1) * LANES]
                     + bias_ref[head, :, c * LANES:(c + 1) * LANES])
                if c == 0 and jj == 0:
                    t = t + pen_first
                if c == 2 and jj == nblk - 1:
                    t = t + pen_last
                sb.append(t)
            row_max = jnp.max(jnp.maximum(jnp.maximum(sb[0], sb[1]), sb[2]), axis=-1, keepdims=True)
            sink = sink_ref[head]
            m = jnp.maximum(row_max, sink)
            ps.append(jnp.concatenate([jnp.exp2(t - m).astype(jnp.bfloat16) for t in sb], axis=1))
            sink_terms.append(jnp.exp2(sink - m))
        return jnp.concatenate(ps, axis=0), sink_terms

    def weighted_values(jj, pp0, p, sink_terms):
        r0 = jj * CHUNK
        kv_tile = pp0 // (N_HEADS // N_KV)
        vd = jnp.concatenate(
            [v_ext[r0:r0 + 3 * CHUNK, kv_tile * LANES:(kv_tile + 1) * LANES], ones], axis=1)
        r = _dot(p, vd)
        outs = []
        for hh in range(4):
            rh = r[hh * CHUNK:(hh + 1) * CHUNK]
            outs.append(rh[:, :LANES] / (rh[:, LANES:] + sink_terms[hh]))
        for pr in range(2):
            pair = jnp.where(low_blk, outs[2 * pr], outs[2 * pr + 1])
            c0 = (pp0 + pr) * LANES
            attn_ref[r0:r0 + CHUNK, c0:c0 + LANES] = pair.astype(jnp.bfloat16)

    steps = [(jj, pp0) for jj in range(nblk) for pp0 in range(0, N_HEADS // 2, 2)]
    s_next = scores(*steps[0])
    for n, step in enumerate(steps):
        s_cur = s_next
        if n + 1 < len(steps):
            s_next = scores(*steps[n + 1])
        p, sink_terms = softmax_numerators(*step, s_cur)
        weighted_values(*step, p, sink_terms)

    yb = _dot(attn_ref[...], wb_ref[...])
    gb = _dot(h, w_ref[:, :D_MODEL])
    mixed = (jax.nn.sigmoid(gb) * yb + ya_ref[...].astype(jnp.float32)).astype(jnp.bfloat16)
    o_ref[...] = x_ref[...] + _dot(mixed, wo_ref[...])


def _attn_mix(x3, h3, ya3, kk3, vv3, w_b, gq, sink, bias, wb, wo):
    b, s, _ = x3.shape
    tm = TOKENS_MIX
    nblk = tm // CHUNK
    nb = s // CHUNK
    kw = N_KV * HEAD_DIM
    tile = lambda bi, i: (bi, i, 0)
    prev = lambda bi, i: (bi, jnp.maximum(i * nblk - 1, 0), 0)
    nxt = lambda bi, i: (bi, jnp.minimum((i + 1) * nblk, nb - 1), 0)
    return pl.pallas_call(
        _attn_mix_kernel,
        grid=(b, s // tm),
        in_specs=[
            pl.BlockSpec((None, tm, D_MODEL), tile),
            pl.BlockSpec((None, tm, D_MODEL), tile),
            pl.BlockSpec((None, tm, D_MODEL), tile),
            pl.BlockSpec((None, CHUNK, kw), prev),
            pl.BlockSpec((None, tm, kw), tile),
            pl.BlockSpec((None, CHUNK, kw), nxt),
            pl.BlockSpec((None, CHUNK, kw), prev),
            pl.BlockSpec((None, tm, kw), tile),
            pl.BlockSpec((None, CHUNK, kw), nxt),
            _const_spec((D_MODEL, 2 * D_MODEL)),
            _const_spec((1, D_MODEL)),
            _const_spec((N_HEADS, 1, LANES)),
            _const_spec((N_HEADS, CHUNK, 3 * CHUNK)),
            _const_spec((D_MODEL, D_MODEL)),
            _const_spec((D_MODEL, D_MODEL)),
        ],
        out_specs=pl.BlockSpec((None, tm, D_MODEL), tile),
        out_shape=jax.ShapeDtypeStruct((b, s, D_MODEL), jnp.float32),
        scratch_shapes=[pltpu.VMEM((tm, D_MODEL), jnp.bfloat16)],
        compiler_params=pltpu.CompilerParams(
            dimension_semantics=("arbitrary", "arbitrary"), vmem_limit_bytes=VMEM_LIMIT_BYTES),
        name="attn_mix",
    )(x3, h3, ya3, kk3, kk3, kk3, vv3, vv3, vv3, w_b, gq, sink, bias, wb, wo)


def _ffn_kernel(x_ref, ln_ref, w1_ref, w2_ref, o_ref):
    x = x_ref[...]
    h = _rmsnorm_rows(x, ln_ref[...]).astype(jnp.bfloat16)
    acc = x
    for c in range(D_FF // FF_CHUNK):
        a = jnp.maximum(_dot(h, w1_ref[:, c * FF_CHUNK:(c + 1) * FF_CHUNK]), 0.0)
        acc = acc + _dot((a * a).astype(jnp.bfloat16), w2_ref[c * FF_CHUNK:(c + 1) * FF_CHUNK, :])
    o_ref[...] = acc


def _ffn(x2, ln_g, w1, w2):
    t = x2.shape[0]
    tm = TOKENS_FFN
    return pl.pallas_call(
        _ffn_kernel,
        grid=(t // tm,),
        in_specs=[
            pl.BlockSpec((tm, D_MODEL), lambda i: (i, 0)),
            _const_spec((1, D_MODEL)),
            _const_spec((D_MODEL, D_FF)),
            _const_spec((D_FF, D_MODEL)),
        ],
        out_specs=pl.BlockSpec((tm, D_MODEL), lambda i: (i, 0)),
        out_shape=jax.ShapeDtypeStruct((t, D_MODEL), jnp.float32),
        compiler_params=pltpu.CompilerParams(
            dimension_semantics=("arbitrary",), vmem_limit_bytes=VMEM_LIMIT_BYTES),
        name="ffn",
    )(x2, ln_g, w1, w2)


def _alibi_bias():
    qi = np.arange(CHUNK)[:, None]
    kj = np.arange(3 * CHUNK)[None, :]
    rel = np.abs(kj - CHUNK - qi).astype(np.float64)
    slopes = np.asarray(_SLOPES, np.float64)[:, None, None] * LOG2E
    bias = np.where(rel <= WINDOW, -slopes * rel, MASKED_LOGIT)
    return jnp.asarray(bias, jnp.float32)


def kernel(x, ln1_g, w_in, a_norm_g, a_w_s, a_b_s, b_q_norm_g, b_k_norm_g, b_sink,
           w_branch_a, w_branch_b, w_out, ln2_g, w_ff1, w_ff2):
    b, s, d = x.shape
    depth = w_in.shape[0]
    assert d == D_MODEL and s % TOKENS_MIX == 0 and (b * s) % TOKENS_GMLP == 0 and (b * s) % TOKENS_FFN == 0
    bf = jnp.bfloat16
    o_ga, o_gb, o_za, o_q, o_k, o_v = 0, D_MODEL, 2 * D_MODEL, 4 * D_MODEL, 5 * D_MODEL, 5 * D_MODEL + N_KV * HEAD_DIM

    w_a = jnp.concatenate([w_in[:, :, o_ga:o_gb], w_in[:, :, o_za:o_q]], axis=-1).astype(bf)
    order = np.asarray(_HEAD_ORDER)
    w_q = w_in[:, :, o_q:o_k].reshape(depth, D_MODEL, N_HEADS, HEAD_DIM)[:, :, order].reshape(depth, D_MODEL, D_MODEL)
    w_b = jnp.concatenate([w_in[:, :, o_gb:o_za], w_q], axis=-1).astype(bf)
    w_kv = w_in[:, :, o_k:].astype(bf)
    gk = jnp.tile(b_k_norm_g, (1, N_KV))[:, None, :]
    gq = (jnp.tile(b_q_norm_g, (1, N_HEADS)) * (HEAD_DIM ** -0.5 * LOG2E))[:, None, :]
    sink = jnp.broadcast_to((b_sink * LOG2E)[:, :, None, None], (depth, N_HEADS, 1, LANES))
    ws = a_w_s.astype(bf)
    bs = jnp.broadcast_to(a_b_s[:, :, :, None], (depth, A_GROUPS, CHUNK, LANES))
    wa, wo = w_branch_a.astype(bf), w_out.astype(bf)
    wb = w_branch_b.reshape(depth, N_HEADS, HEAD_DIM, D_MODEL)[:, order].reshape(depth, D_MODEL, D_MODEL).astype(bf)
    w1, w2 = w_ff1.astype(bf), w_ff2.astype(bf)
    bias = _alibi_bias()

    x2 = x.reshape(b * s, d)
    for l in range(depth):
        ya, h, kk, vv = _gmlp_kv(x2, ln1_g[l][None], w_a[l], w_kv[l], gk[l], a_norm_g[l][None],
                                 ws[l], bs[l], wa[l])
        x3 = _attn_mix(x2.reshape(b, s, d), h.reshape(b, s, d), ya.reshape(b, s, d),
                       kk.reshape(b, s, -1), vv.reshape(b, s, -1),
                       w_b[l], gq[l], sink[l], bias, wb[l], wo[l])
        x2 = _ffn(x3.reshape(b * s, d), ln2_g[l][None], w1[l], w2[l])
    return x2.reshape(b, s, d)
```

```python
import functools

import jax
import jax.numpy as jnp
import numpy as np
from jax import lax
from jax.experimental import pallas as pl
from jax.experimental.pallas import tpu as pltpu

D_MODEL = 1024
CHUNK = 128
A_GROUPS = 8
HEAD_DIM = 64
N_HEADS = 16
N_KV = 4
GQ = N_HEADS // N_KV
WINDOW = 128
D_FF = 4 * D_MODEL
EPS = 1e-6
LANES = 128
MASKED_LOGIT = -1e30
LOG2E = float(np.log2(np.e))

VMEM_LIMIT_BYTES = 56 * 1024 * 1024

TOKENS_GMLP = 1024
TOKENS_MIX = 512
GATE_CHUNKS = 4
TOKENS_FFN = 512
FF_CHUNK = 1024

_SLOPES = [float(2.0 ** (-8.0 * (h + 1.0) / N_HEADS)) for h in range(N_HEADS)]

_PAIR_HEADS = [(2 * GQ * (pp // GQ) + pp % GQ, 2 * GQ * (pp // GQ) + GQ + pp % GQ)
               for pp in range(N_HEADS // 2)]
_HEAD_ORDER = [h for pair in _PAIR_HEADS for h in pair]


def _const_spec(shape, layer=None):
    nd = len(shape)
    if layer is None:
        return pl.BlockSpec(shape, lambda *_: (0,) * nd, pipeline_mode=pl.Buffered(1))
    return pl.BlockSpec((None, *shape), lambda *_: (layer,) + (0,) * nd, pipeline_mode=pl.Buffered(1))


def _rmsnorm_rows(x, g, eps=EPS):
    ms = jnp.mean(x * x, axis=-1, keepdims=True)
    return x * lax.rsqrt(ms + eps) * g


def _gelu_twice(x):
    return x + x * lax.erf(x * np.float32(1.0 / np.sqrt(2.0)))


def _dot(a, b):
    return jnp.dot(a, b, preferred_element_type=jnp.float32)


def _head_pair_sumsq(t, low):
    t2 = t * t
    ssq_all = jnp.sum(t2, axis=-1, keepdims=True)
    ssq_low = jnp.sum(jnp.where(low, t2, 0.0), axis=-1, keepdims=True)
    return jnp.where(low, ssq_low, ssq_all - ssq_low)


def _gmlp_kv_kernel(x_ref, ln_ref, w_ref, wkv_ref, gk_ref, gv_ref, ws_ref, bs_ref, wa_ref,
                    ya_ref, h_ref, k_ref, v_ref, gated_ref):
    half = x_ref.shape[0] // 2
    kw = N_KV * HEAD_DIM
    low = lax.broadcasted_iota(jnp.int32, (half, LANES), 1) < HEAD_DIM

    def project(r):
        rows = pl.ds(r * half, half)
        h = _rmsnorm_rows(x_ref[rows, :], ln_ref[...]).astype(jnp.bfloat16)
        h_ref[rows, :] = h
        u2 = _gelu_twice(_dot(h, w_ref[:, D_MODEL:2 * D_MODEL]))
        v2 = _gelu_twice(_dot(h, w_ref[:, 2 * D_MODEL:]))
        vn = _rmsnorm_rows(v2, gv_ref[...], eps=4.0 * EPS).astype(jnp.bfloat16)
        kv = _dot(h, wkv_ref[...])
        parts = []
        for pp in range(kw // LANES):
            t = kv[:, pp * LANES:(pp + 1) * LANES]
            ssq = _head_pair_sumsq(t, low)
            parts.append(t * lax.rsqrt(ssq * (1.0 / HEAD_DIM) + EPS) * gk_ref[:, pp * LANES:(pp + 1) * LANES])
        k_ref[rows, :] = jnp.concatenate(parts, axis=-1).astype(jnp.bfloat16)
        v_ref[rows, :] = kv[:, kw:].astype(jnp.bfloat16)
        return h, u2, vn

    def gate_and_mix(r, h, u2, vn):
        base = r * half
        for n in range(half // CHUNK):
            r0 = n * CHUNK
            for g in range(A_GROUPS):
                c0 = g * LANES
                s = _dot(ws_ref[g], vn[r0:r0 + CHUNK, c0:c0 + LANES]) + bs_ref[g]
                gated_ref[base + r0:base + r0 + CHUNK, c0:c0 + LANES] = (
                    u2[r0:r0 + CHUNK, c0:c0 + LANES] * s).astype(jnp.bfloat16)
        rows = pl.ds(base, half)
        ya = _dot(gated_ref[rows, :], wa_ref[...])
        ga = _dot(h, w_ref[:, :D_MODEL])
        ya_ref[rows, :] = (jax.nn.sigmoid(ga) * ya).astype(jnp.bfloat16)

    first = project(0)
    second = project(1)
    gate_and_mix(0, *first)
    gate_and_mix(1, *second)


def _gmlp_kv(layer, x2, ln_g, w_a, w_kv, gk, gv, ws, bs, wa):
    t = x2.shape[0]
    tm = TOKENS_GMLP
    kw = N_KV * HEAD_DIM
    rows = lambda width: pl.BlockSpec((tm, width), lambda i: (i, 0))
    return pl.pallas_call(
        _gmlp_kv_kernel,
        grid=(t // tm,),
        in_specs=[
            rows(D_MODEL),
            _const_spec((1, D_MODEL), layer),
            _const_spec((D_MODEL, 3 * D_MODEL), layer),
            _const_spec((D_MODEL, 2 * kw), layer),
            _const_spec((1, kw), layer),
            _const_spec((1, D_MODEL), layer),
            _const_spec((A_GROUPS, CHUNK, CHUNK), layer),
            _const_spec((A_GROUPS, CHUNK, LANES), layer),
            _const_spec((D_MODEL, D_MODEL), layer),
        ],
        out_specs=[rows(D_MODEL), rows(D_MODEL), rows(kw), rows(kw)],
        out_shape=[
            jax.ShapeDtypeStruct((t, D_MODEL), jnp.bfloat16),
            jax.ShapeDtypeStruct((t, D_MODEL), jnp.bfloat16),
            jax.ShapeDtypeStruct((t, kw), jnp.bfloat16),
            jax.ShapeDtypeStruct((t, kw), jnp.bfloat16),
        ],
        scratch_shapes=[pltpu.VMEM((tm, D_MODEL), jnp.bfloat16)],
        compiler_params=pltpu.CompilerParams(
            dimension_semantics=("arbitrary",), vmem_limit_bytes=VMEM_LIMIT_BYTES),
        name="gmlp_kv",
    )(x2, ln_g, w_a, w_kv, gk, gv, ws, bs, wa)


def _attn_mix_kernel(x_ref, h_ref, ya_ref, kp_ref, kc_ref, kn_ref, vp_ref, vc_ref, vn_ref,
                     w_ref, gq_ref, sink_ref, bias_ref, wb_ref, wo_ref,
                     o_ref, attn_ref, gate_ref):
    rows = x_ref.shape[0]
    nblk = rows // CHUNK
    i = pl.program_id(1)
    last = pl.num_programs(1) - 1

    h = h_ref[...]

    low = lax.broadcasted_iota(jnp.int32, (rows, LANES), 1) < HEAD_DIM
    q_low, q_high = {}, {}

    def project_q(pp0):
        c0 = pp0 * LANES
        q = _dot(h, w_ref[:, D_MODEL + c0:D_MODEL + c0 + 2 * LANES])
        for pp in (pp0, pp0 + 1):
            t = q[:, (pp - pp0) * LANES:(pp - pp0 + 1) * LANES]
            ssq = _head_pair_sumsq(t, low)
            qn = t * lax.rsqrt(ssq * (1.0 / HEAD_DIM) + EPS) * gq_ref[:, pp * LANES:(pp + 1) * LANES]
            q_low[pp] = jnp.where(low, qn, 0.0).astype(jnp.bfloat16)
            q_high[pp] = jnp.where(low, 0.0, qn).astype(jnp.bfloat16)

    k_ext = jnp.concatenate([kp_ref[...], kc_ref[...], kn_ref[...]], axis=0)
    v_ext = jnp.concatenate([vp_ref[...], vc_ref[...], vn_ref[...]], axis=0)
    ones = jnp.ones((3 * CHUNK, LANES), jnp.bfloat16)
    low_blk = lax.broadcasted_iota(jnp.int32, (CHUNK, LANES), 1) < HEAD_DIM
    pen_first = jnp.where(i == 0, MASKED_LOGIT, 0.0).astype(jnp.float32)
    pen_last = jnp.where(i == last, MASKED_LOGIT, 0.0).astype(jnp.float32)

    def scores(jj, pp0):
        r0 = jj * CHUNK
        kv_tile = pp0 // (N_HEADS // N_KV)
        lhs = jnp.concatenate(
            [qh[pp][r0:r0 + CHUNK] for pp in (pp0, pp0 + 1) for qh in (q_low, q_high)], axis=0)
        kd = k_ext[r0:r0 + 3 * CHUNK, kv_tile * LANES:(kv_tile + 1) * LANES]
        return lax.dot_general(lhs, kd, (((1,), (1,)), ((), ())),
                               preferred_element_type=jnp.float32)

    def softmax_numerators(jj, pp0, s):
        ps, sink_terms = [], []
        for hh, head in enumerate(_PAIR_HEADS[pp0] + _PAIR_HEADS[pp0 + 1]):
            sb = []
            for c in range(3):
                t = (s[hh * CHUNK:(hh + 1) * CHUNK, c * LANES:(c + 1) * LANES]
                     + bias_ref[head, :, c * LANES:(c + 1) * LANES])
                if c == 0 and jj == 0:
                    t = t + pen_first
                if c == 2 and jj == nblk - 1:
                    t = t + pen_last
                sb.append(t)
            row_max = jnp.max(jnp.maximum(jnp.maximum(sb[0], sb[1]), sb[2]), axis=-1, keepdims=True)
            sink = sink_ref[head]
            m = jnp.maximum(row_max, sink)
            ps.append(jnp.concatenate([jnp.exp2(t - m).astype(jnp.bfloat16) for t in sb], axis=1))
            sink_terms.append(jnp.exp2(sink - m))
        return jnp.concatenate(ps, axis=0), sink_terms

    def weighted_values(jj, pp0, p, sink_terms):
        r0 = jj * CHUNK
        kv_tile = pp0 // (N_HEADS // N_KV)
        vd = jnp.concatenate(
            [v_ext[r0:r0 + 3 * CHUNK, kv_tile * LANES:(kv_tile + 1) * LANES], ones], axis=1)
        r = _dot(p, vd)
        outs = []
        for hh in range(4):
            rh = r[hh * CHUNK:(hh + 1) * CHUNK]
            outs.append(rh[:, :LANES] / (rh[:, LANES:] + sink_terms[hh]))
        for pr in range(2):
            pair = jnp.where(low_blk, outs[2 * pr], outs[2 * pr + 1])
            c0 = (pp0 + pr) * LANES
            attn_ref[r0:r0 + CHUNK, c0:c0 + LANES] = pair.astype(jnp.bfloat16)

    pp0s = list(range(0, N_HEADS // 2, 2))
    for pp0 in pp0s:
        project_q(pp0)
    steps = [(jj, pp0) for jj in range(nblk) for pp0 in pp0s]
    gate_every = len(steps) // GATE_CHUNKS
    gate_cols = D_MODEL // GATE_CHUNKS
    s_next = scores(*steps[0])
    for n, step in enumerate(steps):
        s_cur = s_next
        if n + 1 < len(steps):
            s_next = scores(*steps[n + 1])
        p, sink_terms = softmax_numerators(*step, s_cur)
        if n % gate_every == gate_every - 1:
            c0 = (n // gate_every) * gate_cols
            gate_ref[:, c0:c0 + gate_cols] = _dot(h, w_ref[:, c0:c0 + gate_cols])
        weighted_values(*step, p, sink_terms)

    yb = _dot(attn_ref[...], wb_ref[...])
    mixed = (jax.nn.sigmoid(gate_ref[...]) * yb + ya_ref[...].astype(jnp.float32)).astype(jnp.bfloat16)
    o_ref[...] = x_ref[...] + _dot(mixed, wo_ref[...])


def _attn_mix(layer, x3, h3, ya3, kk3, vv3, w_b, gq, sink, bias, wb, wo):
    b, s, _ = x3.shape
    tm = TOKENS_MIX
    nblk = tm // CHUNK
    nb = s // CHUNK
    kw = N_KV * HEAD_DIM
    tile = lambda bi, i: (bi, i, 0)
    prev = lambda bi, i: (bi, jnp.maximum(i * nblk - 1, 0), 0)
    nxt = lambda bi, i: (bi, jnp.minimum((i + 1) * nblk, nb - 1), 0)
    return pl.pallas_call(
        _attn_mix_kernel,
        grid=(b, s // tm),
        in_specs=[
            pl.BlockSpec((None, tm, D_MODEL), tile),
            pl.BlockSpec((None, tm, D_MODEL), tile),
            pl.BlockSpec((None, tm, D_MODEL), tile),
            pl.BlockSpec((None, CHUNK, kw), prev),
            pl.BlockSpec((None, tm, kw), tile),
            pl.BlockSpec((None, CHUNK, kw), nxt),
            pl.BlockSpec((None, CHUNK, kw), prev),
            pl.BlockSpec((None, tm, kw), tile),
            pl.BlockSpec((None, CHUNK, kw), nxt),
            _const_spec((D_MODEL, 2 * D_MODEL), layer),
            _const_spec((1, D_MODEL), layer),
            _const_spec((N_HEADS, 1, LANES), layer),
            _const_spec((N_HEADS, CHUNK, 3 * CHUNK)),
            _const_spec((D_MODEL, D_MODEL), layer),
            _const_spec((D_MODEL, D_MODEL), layer),
        ],
        out_specs=pl.BlockSpec((None, tm, D_MODEL), tile),
        out_shape=jax.ShapeDtypeStruct((b, s, D_MODEL), jnp.float32),
        scratch_shapes=[pltpu.VMEM((tm, D_MODEL), jnp.bfloat16), pltpu.VMEM((tm, D_MODEL), jnp.float32)],
        compiler_params=pltpu.CompilerParams(
            dimension_semantics=("arbitrary", "arbitrary"), vmem_limit_bytes=VMEM_LIMIT_BYTES),
        name="attn_mix",
    )(x3, h3, ya3, kk3, kk3, kk3, vv3, vv3, vv3, w_b, gq, sink, bias, wb, wo)


def _ffn_kernel(x_ref, ln_ref, w1_ref, w2_ref, o_ref):
    x = x_ref[...]
    h = _rmsnorm_rows(x, ln_ref[...]).astype(jnp.bfloat16)
    acc = x
    for c in range(D_FF // FF_CHUNK):
        a = jnp.maximum(_dot(h, w1_ref[:, c * FF_CHUNK:(c + 1) * FF_CHUNK]), 0.0)
        acc = acc + _dot((a * a).astype(jnp.bfloat16), w2_ref[c * FF_CHUNK:(c + 1) * FF_CHUNK, :])
    o_ref[...] = acc


def _ffn(layer, x2, ln_g, w1, w2):
    t = x2.shape[0]
    tm = TOKENS_FFN
    return pl.pallas_call(
        _ffn_kernel,
        grid=(t // tm,),
        in_specs=[
            pl.BlockSpec((tm, D_MODEL), lambda i: (i, 0)),
            _const_spec((1, D_MODEL), layer),
            _const_spec((D_MODEL, D_FF), layer),
            _const_spec((D_FF, D_MODEL), layer),
        ],
        out_specs=pl.BlockSpec((tm, D_MODEL), lambda i: (i, 0)),
        out_shape=jax.ShapeDtypeStruct((t, D_MODEL), jnp.float32),
        compiler_params=pltpu.CompilerParams(
            dimension_semantics=("arbitrary",), vmem_limit_bytes=VMEM_LIMIT_BYTES),
        name="ffn",
    )(x2, ln_g, w1, w2)


def _alibi_bias():
    qi = np.arange(CHUNK)[:, None]
    kj = np.arange(3 * CHUNK)[None, :]
    rel = np.abs(kj - CHUNK - qi).astype(np.float64)
    slopes = np.asarray(_SLOPES, np.float64)[:, None, None] * LOG2E
    bias = np.where(rel <= WINDOW, -slopes * rel, MASKED_LOGIT)
    return jnp.asarray(bias, jnp.float32)


def kernel(x, ln1_g, w_in, a_norm_g, a_w_s, a_b_s, b_q_norm_g, b_k_norm_g, b_sink,
           w_branch_a, w_branch_b, w_out, ln2_g, w_ff1, w_ff2):
    b, s, d = x.shape
    depth = w_in.shape[0]
    assert d == D_MODEL and s % TOKENS_MIX == 0 and (b * s) % TOKENS_GMLP == 0 and (b * s) % TOKENS_FFN == 0
    bf = jnp.bfloat16
    o_ga, o_gb, o_za, o_q, o_k, o_v = 0, D_MODEL, 2 * D_MODEL, 4 * D_MODEL, 5 * D_MODEL, 5 * D_MODEL + N_KV * HEAD_DIM

    w_a = jnp.concatenate([w_in[:, :, o_ga:o_gb], w_in[:, :, o_za:o_q]], axis=-1).astype(bf)
    order = np.asarray(_HEAD_ORDER)
    w_q = w_in[:, :, o_q:o_k].reshape(depth, D_MODEL, N_HEADS, HEAD_DIM)[:, :, order].reshape(depth, D_MODEL, D_MODEL)
    w_b = jnp.concatenate([w_in[:, :, o_gb:o_za], w_q], axis=-1).astype(bf)
    w_kv = w_in[:, :, o_k:].astype(bf)
    gk = jnp.tile(b_k_norm_g, (1, N_KV))[:, None, :]
    gq = (jnp.tile(b_q_norm_g, (1, N_HEADS)) * (HEAD_DIM ** -0.5 * LOG2E))[:, None, :]
    sink = jnp.broadcast_to((b_sink * LOG2E)[:, :, None, None], (depth, N_HEADS, 1, LANES))
    ws = (0.5 * a_w_s).astype(bf)
    bs = jnp.broadcast_to(0.5 * a_b_s[:, :, :, None], (depth, A_GROUPS, CHUNK, LANES))
    wa, wo = w_branch_a.astype(bf), w_out.astype(bf)
    wb = w_branch_b.reshape(depth, N_HEADS, HEAD_DIM, D_MODEL)[:, order].reshape(depth, D_MODEL, D_MODEL).astype(bf)
    w1, w2 = w_ff1.astype(bf), w_ff2.astype(bf)
    bias = _alibi_bias()
    ln1, ln2, gv = ln1_g[:, None, :], ln2_g[:, None, :], a_norm_g[:, None, :]

    x2 = x.reshape(b * s, d)
    for l in range(depth):
        ya, h, kk, vv = _gmlp_kv(l, x2, ln1, w_a, w_kv, gk, gv, ws, bs, wa)
        x3 = _attn_mix(l, x2.reshape(b, s, d), h.reshape(b, s, d), ya.reshape(b, s, d),
                       kk.reshape(b, s, -1), vv.reshape(b, s, -1),
                       w_b, gq, sink, bias, wb, wo)
        x2 = _ffn(l, x3.reshape(b * s, d), ln2, w1, w2)
    return x2.reshape(b, s, d)
```

```python
import jax
import jax.numpy as jnp
import numpy as np
from jax import lax
from jax.experimental import pallas as pl
from jax.experimental.pallas import tpu as pltpu

D_MODEL = 1024
CHUNK = 128
A_GROUPS = 8
HEAD_DIM = 64
N_HEADS = 16
N_KV = 4
GQ = N_HEADS // N_KV
WINDOW = 128
D_FF = 4 * D_MODEL
EPS = 1e-6
LANES = 128
MASKED_LOGIT = -1e30
LOG2E = float(np.log2(np.e))

VMEM_LIMIT_BYTES = 56 * 1024 * 1024

TOKENS_GMLP = 1024
TOKENS_MIX = 512
GATE_CHUNKS = 4
FF_CHUNK = 1024

_SLOPES = [float(2.0 ** (-8.0 * (h + 1.0) / N_HEADS)) for h in range(N_HEADS)]

_PAIR_HEADS = [(2 * GQ * (pp // GQ) + pp % GQ, 2 * GQ * (pp // GQ) + GQ + pp % GQ)
               for pp in range(N_HEADS // 2)]
_HEAD_ORDER = [h for pair in _PAIR_HEADS for h in pair]


def _const_spec(shape, layer=None):
    nd = len(shape)
    if layer is None:
        return pl.BlockSpec(shape, lambda *_: (0,) * nd, pipeline_mode=pl.Buffered(1))
    return pl.BlockSpec((None, *shape), lambda *_: (layer,) + (0,) * nd, pipeline_mode=pl.Buffered(1))


def _rmsnorm_rows(x, g, eps=EPS):
    ms = jnp.mean(x * x, axis=-1, keepdims=True)
    return x * lax.rsqrt(ms + eps) * g


def _gelu_twice(x):
    return x + x * lax.erf(x * np.float32(1.0 / np.sqrt(2.0)))


def _dot(a, b):
    return jnp.dot(a, b, preferred_element_type=jnp.float32)


def _head_pair_sumsq(t, low):
    t2 = t * t
    ssq_all = jnp.sum(t2, axis=-1, keepdims=True)
    ssq_low = jnp.sum(jnp.where(low, t2, 0.0), axis=-1, keepdims=True)
    return jnp.where(low, ssq_low, ssq_all - ssq_low)


def _gmlp_kv_kernel(x_ref, ln_ref, w_ref, wkv_ref, gk_ref, gv_ref, ws_ref, bs_ref, wa_ref,
                    ya_ref, h_ref, kv_ref, gated_ref):
    half = x_ref.shape[0] // 2
    kw = N_KV * HEAD_DIM
    low = lax.broadcasted_iota(jnp.int32, (half, LANES), 1) < HEAD_DIM

    def project(r):
        rows = pl.ds(r * half, half)
        h = _rmsnorm_rows(x_ref[rows, :], ln_ref[...]).astype(jnp.bfloat16)
        h_ref[rows, :] = h
        u2 = _gelu_twice(_dot(h, w_ref[:, D_MODEL:2 * D_MODEL]))
        v2 = _gelu_twice(_dot(h, w_ref[:, 2 * D_MODEL:]))
        vn = _rmsnorm_rows(v2, gv_ref[...], eps=4.0 * EPS).astype(jnp.bfloat16)
        kv = _dot(h, wkv_ref[...])
        parts = []
        for pp in range(kw // LANES):
            t = kv[:, pp * LANES:(pp + 1) * LANES]
            ssq = _head_pair_sumsq(t, low)
            parts.append(t * lax.rsqrt(ssq * (1.0 / HEAD_DIM) + EPS) * gk_ref[:, pp * LANES:(pp + 1) * LANES])
        kv_ref[rows, :] = jnp.concatenate(parts + [kv[:, kw:]], axis=-1).astype(jnp.bfloat16)
        return h, u2, vn

    def gate_and_mix(r, h, u2, vn):
        base = r * half
        for n in range(half // CHUNK):
            r0 = n * CHUNK
            for g in range(A_GROUPS):
                c0 = g * LANES
                s = _dot(ws_ref[g], vn[r0:r0 + CHUNK, c0:c0 + LANES]) + bs_ref[g]
                gated_ref[base + r0:base + r0 + CHUNK, c0:c0 + LANES] = (
                    u2[r0:r0 + CHUNK, c0:c0 + LANES] * s).astype(jnp.bfloat16)
        rows = pl.ds(base, half)
        ya = _dot(gated_ref[rows, :], wa_ref[...])
        ga = _dot(h, w_ref[:, :D_MODEL])
        ya_ref[rows, :] = (jax.nn.sigmoid(ga) * ya).astype(jnp.bfloat16)

    first = project(0)
    second = project(1)
    gate_and_mix(0, *first)
    gate_and_mix(1, *second)


def _gmlp_kv(layer, x2, ln_g, w_a, w_kv, gk, gv, ws, bs, wa):
    t = x2.shape[0]
    tm = TOKENS_GMLP
    kw = N_KV * HEAD_DIM
    rows = lambda width: pl.BlockSpec((tm, width), lambda i: (i, 0))
    return pl.pallas_call(
        _gmlp_kv_kernel,
        grid=(t // tm,),
        in_specs=[
            rows(D_MODEL),
            _const_spec((1, D_MODEL), layer),
            _const_spec((D_MODEL, 3 * D_MODEL), layer),
            _const_spec((D_MODEL, 2 * kw), layer),
            _const_spec((1, kw), layer),
            _const_spec((1, D_MODEL), layer),
            _const_spec((A_GROUPS, CHUNK, CHUNK), layer),
            _const_spec((A_GROUPS, CHUNK, LANES), layer),
            _const_spec((D_MODEL, D_MODEL), layer),
        ],
        out_specs=[rows(D_MODEL), rows(D_MODEL), rows(2 * kw)],
        out_shape=[
            jax.ShapeDtypeStruct((t, D_MODEL), jnp.bfloat16),
            jax.ShapeDtypeStruct((t, D_MODEL), jnp.bfloat16),
            jax.ShapeDtypeStruct((t, 2 * kw), jnp.bfloat16),
        ],
        scratch_shapes=[pltpu.VMEM((tm, D_MODEL), jnp.bfloat16)],
        compiler_params=pltpu.CompilerParams(
            dimension_semantics=("arbitrary",), vmem_limit_bytes=VMEM_LIMIT_BYTES),
        name="gmlp_kv",
    )(x2, ln_g, w_a, w_kv, gk, gv, ws, bs, wa)


def _attn_ffn_kernel(x_ref, h_ref, ya_ref, kvp_ref, kvc_ref, kvn_ref,
                     w_ref, gq_ref, sink_ref, bias_ref, wb_ref, wo_ref, ln2_ref, w1_ref, w2_ref,
                     o_ref, attn_ref, gate_ref):
    rows = x_ref.shape[0]
    nblk = rows // CHUNK
    i = pl.program_id(1)
    last = pl.num_programs(1) - 1

    h = h_ref[...]

    low = lax.broadcasted_iota(jnp.int32, (rows, LANES), 1) < HEAD_DIM
    q_low, q_high = {}, {}

    def project_q(pp0):
        c0 = pp0 * LANES
        q = _dot(h, w_ref[:, D_MODEL + c0:D_MODEL + c0 + 2 * LANES])
        for pp in (pp0, pp0 + 1):
            t = q[:, (pp - pp0) * LANES:(pp - pp0 + 1) * LANES]
            ssq = _head_pair_sumsq(t, low)
            qn = t * lax.rsqrt(ssq * (1.0 / HEAD_DIM) + EPS) * gq_ref[:, pp * LANES:(pp + 1) * LANES]
            q_low[pp] = jnp.where(low, qn, 0.0).astype(jnp.bfloat16)
            q_high[pp] = jnp.where(low, 0.0, qn).astype(jnp.bfloat16)

    kv_ext = jnp.concatenate([kvp_ref[...], kvc_ref[...], kvn_ref[...]], axis=0)
    k_ext, v_ext = kv_ext[:, :N_KV * HEAD_DIM], kv_ext[:, N_KV * HEAD_DIM:]
    ones = jnp.ones((3 * CHUNK, LANES), jnp.bfloat16)
    low_blk = lax.broadcasted_iota(jnp.int32, (CHUNK, LANES), 1) < HEAD_DIM
    pen_first = jnp.where(i == 0, MASKED_LOGIT, 0.0).astype(jnp.float32)
    pen_last = jnp.where(i == last, MASKED_LOGIT, 0.0).astype(jnp.float32)

    def scores(jj, pp0):
        r0 = jj * CHUNK
        kv_tile = pp0 // (N_HEADS // N_KV)
        lhs = jnp.concatenate(
            [qh[pp][r0:r0 + CHUNK] for pp in (pp0, pp0 + 1) for qh in (q_low, q_high)], axis=0)
        kd = k_ext[r0:r0 + 3 * CHUNK, kv_tile * LANES:(kv_tile + 1) * LANES]
        return lax.dot_general(lhs, kd, (((1,), (1,)), ((), ())),
                               preferred_element_type=jnp.float32)

    def softmax_numerators(jj, pp0, s):
        ps, sink_terms = [], []
        for hh, head in enumerate(_PAIR_HEADS[pp0] + _PAIR_HEADS[pp0 + 1]):
            sb = []
            for c in range(3):
                t = (s[hh * CHUNK:(hh + 1) * CHUNK, c * LANES:(c + 1) * LANES]
                     + bias_ref[head, :, c * LANES:(c + 1) * LANES])
                if c == 0 and jj == 0:
                    t = t + pen_first
                if c == 2 and jj == nblk - 1:
                    t = t + pen_last
                sb.append(t)
            row_max = jnp.max(jnp.maximum(jnp.maximum(sb[0], sb[1]), sb[2]), axis=-1, keepdims=True)
            sink = sink_ref[head]
            m = jnp.maximum(row_max, sink)
            ps.append(jnp.concatenate([jnp.exp2(t - m).astype(jnp.bfloat16) for t in sb], axis=1))
            sink_terms.append(jnp.exp2(sink - m))
        return jnp.concatenate(ps, axis=0), sink_terms

    def weighted_values(jj, pp0, p, sink_terms):
        r0 = jj * CHUNK
        kv_tile = pp0 // (N_HEADS // N_KV)
        vd = jnp.concatenate(
            [v_ext[r0:r0 + 3 * CHUNK, kv_tile * LANES:(kv_tile + 1) * LANES], ones], axis=1)
        r = _dot(p, vd)
        outs = []
        for hh in range(4):
            rh = r[hh * CHUNK:(hh + 1) * CHUNK]
            outs.append(rh[:, :LANES] / (rh[:, LANES:] + sink_terms[hh]))
        for pr in range(2):
            pair = jnp.where(low_blk, outs[2 * pr], outs[2 * pr + 1])
            c0 = (pp0 + pr) * LANES
            attn_ref[r0:r0 + CHUNK, c0:c0 + LANES] = pair.astype(jnp.bfloat16)

    pp0s = list(range(0, N_HEADS // 2, 2))
    for pp0 in pp0s:
        project_q(pp0)
    steps = [(jj, pp0) for jj in range(nblk) for pp0 in pp0s]
    gate_every = len(steps) // GATE_CHUNKS
    gate_cols = D_MODEL // GATE_CHUNKS
    s_next = scores(*steps[0])
    for n, step in enumerate(steps):
        s_cur = s_next
        if n + 1 < len(steps):
            s_next = scores(*steps[n + 1])
        p, sink_terms = softmax_numerators(*step, s_cur)
        if n % gate_every == gate_every - 1:
            c0 = (n // gate_every) * gate_cols
            gate_ref[:, c0:c0 + gate_cols] = _dot(h, w_ref[:, c0:c0 + gate_cols])
        weighted_values(*step, p, sink_terms)

    yb = _dot(attn_ref[...], wb_ref[...])
    mixed = (jax.nn.sigmoid(gate_ref[...]) * yb + ya_ref[...].astype(jnp.float32)).astype(jnp.bfloat16)
    x_mid = x_ref[...] + _dot(mixed, wo_ref[...])
    o_ref[...] = _ffn_rows(x_mid, ln2_ref, w1_ref, w2_ref)


def _attn_ffn(layer, x3, h3, ya3, kv3, w_b, gq, sink, bias, wb, wo, ln2, w1, w2):
    b, s, _ = x3.shape
    tm = TOKENS_MIX
    nblk = tm // CHUNK
    nb = s // CHUNK
    kw = 2 * N_KV * HEAD_DIM
    tile = lambda bi, i: (bi, i, 0)
    prev = lambda bi, i: (bi, jnp.maximum(i * nblk - 1, 0), 0)
    nxt = lambda bi, i: (bi, jnp.minimum((i + 1) * nblk, nb - 1), 0)
    return pl.pallas_call(
        _attn_ffn_kernel,
        grid=(b, s // tm),
        in_specs=[
            pl.BlockSpec((None, tm, D_MODEL), tile),
            pl.BlockSpec((None, tm, D_MODEL), tile),
            pl.BlockSpec((None, tm, D_MODEL), tile),
            pl.BlockSpec((None, CHUNK, kw), prev),
            pl.BlockSpec((None, tm, kw), tile),
            pl.BlockSpec((None, CHUNK, kw), nxt),
            _const_spec((D_MODEL, 2 * D_MODEL), layer),
            _const_spec((1, D_MODEL), layer),
            _const_spec((N_HEADS, 1, LANES), layer),
            _const_spec((N_HEADS, CHUNK, 3 * CHUNK)),
            _const_spec((D_MODEL, D_MODEL), layer),
            _const_spec((D_MODEL, D_MODEL), layer),
            _const_spec((1, D_MODEL), layer),
            _const_spec((D_MODEL, D_FF), layer),
            _const_spec((D_FF, D_MODEL), layer),
        ],
        out_specs=pl.BlockSpec((None, tm, D_MODEL), tile),
        out_shape=jax.ShapeDtypeStruct((b, s, D_MODEL), jnp.float32),
        scratch_shapes=[pltpu.VMEM((tm, D_MODEL), jnp.bfloat16), pltpu.VMEM((tm, D_MODEL), jnp.float32)],
        compiler_params=pltpu.CompilerParams(
            dimension_semantics=("arbitrary", "arbitrary"), vmem_limit_bytes=VMEM_LIMIT_BYTES),
        name="attn_ffn",
    )(x3, h3, ya3, kv3, kv3, kv3, w_b, gq, sink, bias, wb, wo, ln2, w1, w2)


def _ffn_rows(x, ln_ref, w1_ref, w2_ref):
    h = _rmsnorm_rows(x, ln_ref[...]).astype(jnp.bfloat16)
    acc = x
    for c in range(D_FF // FF_CHUNK):
        a = jnp.maximum(_dot(h, w1_ref[:, c * FF_CHUNK:(c + 1) * FF_CHUNK]), 0.0)
        acc = acc + _dot((a * a).astype(jnp.bfloat16), w2_ref[c * FF_CHUNK:(c + 1) * FF_CHUNK, :])
    return acc


def _alibi_bias():
    qi = np.arange(CHUNK)[:, None]
    kj = np.arange(3 * CHUNK)[None, :]
    rel = np.abs(kj - CHUNK - qi).astype(np.float64)
    slopes = np.asarray(_SLOPES, np.float64)[:, None, None] * LOG2E
    bias = np.where(rel <= WINDOW, -slopes * rel, MASKED_LOGIT)
    return jnp.asarray(bias, jnp.float32)


def kernel(x, ln1_g, w_in, a_norm_g, a_w_s, a_b_s, b_q_norm_g, b_k_norm_g, b_sink,
           w_branch_a, w_branch_b, w_out, ln2_g, w_ff1, w_ff2):
    b, s, d = x.shape
    depth = w_in.shape[0]
    assert d == D_MODEL and s % TOKENS_MIX == 0 and (b * s) % TOKENS_GMLP == 0
    bf = jnp.bfloat16
    o_ga, o_gb, o_za, o_q, o_k, o_v = 0, D_MODEL, 2 * D_MODEL, 4 * D_MODEL, 5 * D_MODEL, 5 * D_MODEL + N_KV * HEAD_DIM

    w_a = jnp.concatenate([w_in[:, :, o_ga:o_gb], w_in[:, :, o_za:o_q]], axis=-1).astype(bf)
    order = np.asarray(_HEAD_ORDER)
    w_q = w_in[:, :, o_q:o_k].reshape(depth, D_MODEL, N_HEADS, HEAD_DIM)[:, :, order].reshape(depth, D_MODEL, D_MODEL)
    w_b = jnp.concatenate([w_in[:, :, o_gb:o_za], w_q], axis=-1).astype(bf)
    w_kv = w_in[:, :, o_k:].astype(bf)
    gk = jnp.tile(b_k_norm_g, (1, N_KV))[:, None, :]
    gq = (jnp.tile(b_q_norm_g, (1, N_HEADS)) * (HEAD_DIM ** -0.5 * LOG2E))[:, None, :]
    sink = jnp.broadcast_to((b_sink * LOG2E)[:, :, None, None], (depth, N_HEADS, 1, LANES))
    ws = (0.5 * a_w_s).astype(bf)
    bs = jnp.broadcast_to(0.5 * a_b_s[:, :, :, None], (depth, A_GROUPS, CHUNK, LANES))
    wa, wo = w_branch_a.astype(bf), w_out.astype(bf)
    wb = w_branch_b.reshape(depth, N_HEADS, HEAD_DIM, D_MODEL)[:, order].reshape(depth, D_MODEL, D_MODEL).astype(bf)
    w1, w2 = w_ff1.astype(bf), w_ff2.astype(bf)
    bias = _alibi_bias()
    ln1, ln2, gv = ln1_g[:, None, :], ln2_g[:, None, :], a_norm_g[:, None, :]

    x2 = x.reshape(b * s, d)
    for l in range(depth):
        ya, h, kv = _gmlp_kv(l, x2, ln1, w_a, w_kv, gk, gv, ws, bs, wa)
        x3 = _attn_ffn(l, x2.reshape(b, s, d), h.reshape(b, s, d), ya.reshape(b, s, d),
                       kv.reshape(b, s, -1), w_b, gq, sink, bias, wb, wo, ln2, w1, w2)
        x2 = x3.reshape(b * s, d)
    return x2.reshape(b, s, d)
```

```python
import jax
import jax.numpy as jnp
import numpy as np
from jax import lax
from jax.experimental import pallas as pl
from jax.experimental.pallas import tpu as pltpu

D_MODEL = 1024
CHUNK = 128
A_GROUPS = 8
HEAD_DIM = 64
N_HEADS = 16
N_KV = 4
GQ = N_HEADS // N_KV
WINDOW = 128
D_FF = 4 * D_MODEL
EPS = 1e-6
LANES = 128
MASKED_LOGIT = -1e30
LOG2E = float(np.log2(np.e))

VMEM_LIMIT_BYTES = 56 * 1024 * 1024

TOKENS_GMLP = 1024
TOKENS_MIX = 512
GATE_CHUNKS = 4
FF_CHUNK = 1024

_SLOPES = [float(2.0 ** (-8.0 * (h + 1.0) / N_HEADS)) for h in range(N_HEADS)]

_PAIR_HEADS = [(2 * GQ * (pp // GQ) + pp % GQ, 2 * GQ * (pp // GQ) + GQ + pp % GQ)
               for pp in range(N_HEADS // 2)]
_HEAD_ORDER = [h for pair in _PAIR_HEADS for h in pair]


def _const_spec(shape, layer=None):
    nd = len(shape)
    if layer is None:
        return pl.BlockSpec(shape, lambda *_: (0,) * nd, pipeline_mode=pl.Buffered(1))
    return pl.BlockSpec((None, *shape), lambda *_: (layer,) + (0,) * nd, pipeline_mode=pl.Buffered(1))


def _rmsnorm_rows(x, g, eps=EPS):
    ms = jnp.mean(x * x, axis=-1, keepdims=True)
    return x * lax.rsqrt(ms + eps) * g


def _gelu_twice(x):
    return x + x * lax.erf(x * np.float32(1.0 / np.sqrt(2.0)))


def _dot(a, b):
    return jnp.dot(a, b, preferred_element_type=jnp.float32)


def _head_pair_sumsq(t, low):
    t2 = t * t
    ssq_all = jnp.sum(t2, axis=-1, keepdims=True)
    ssq_low = jnp.sum(jnp.where(low, t2, 0.0), axis=-1, keepdims=True)
    return jnp.where(low, ssq_low, ssq_all - ssq_low)


_GMLP_ORDER = [("norm", 0), ("v", 0), ("gate", 0), ("u", 0), ("kv", 0),
               ("norm", 1), ("v", 1), ("gate", 1), ("u", 1), ("spatial", 0), ("kv", 1), ("mix", 0),
               ("spatial", 1), ("mix", 1)]


def _gmlp_kv_kernel(x_ref, ln_ref, w_ref, wkv_ref, gk_ref, gv_ref, ws_ref, bs_ref, wa_ref,
                    ya_ref, h_ref, kv_ref, gated_ref):
    half = x_ref.shape[0] // 2
    kw = N_KV * HEAD_DIM
    low = lax.broadcasted_iota(jnp.int32, (half, LANES), 1) < HEAD_DIM

    st = [{}, {}]

    def norm(r):
        rows = pl.ds(r * half, half)
        st[r]["h"] = _rmsnorm_rows(x_ref[rows, :], ln_ref[...]).astype(jnp.bfloat16)
        h_ref[rows, :] = st[r]["h"]

    def proj_v(r):
        v2 = _gelu_twice(_dot(st[r]["h"], w_ref[:, 2 * D_MODEL:]))
        st[r]["vn"] = _rmsnorm_rows(v2, gv_ref[...], eps=4.0 * EPS).astype(jnp.bfloat16)

    def proj_gate(r):
        st[r]["ga"] = _dot(st[r]["h"], w_ref[:, :D_MODEL])

    def proj_u(r):
        st[r]["u2"] = _gelu_twice(_dot(st[r]["h"], w_ref[:, D_MODEL:2 * D_MODEL]))

    def proj_kv(r):
        kv = _dot(st[r]["h"], wkv_ref[...])
        parts = []
        for pp in range(kw // LANES):
            t = kv[:, pp * LANES:(pp + 1) * LANES]
            ssq = _head_pair_sumsq(t, low)
            parts.append(t * lax.rsqrt(ssq * (1.0 / HEAD_DIM) + EPS) * gk_ref[:, pp * LANES:(pp + 1) * LANES])
        kv_ref[pl.ds(r * half, half), :] = jnp.concatenate(parts + [kv[:, kw:]], axis=-1).astype(jnp.bfloat16)

    def spatial(r):
        base = r * half
        vn, u2 = st[r]["vn"], st[r]["u2"]
        for n in range(half // CHUNK):
            r0 = n * CHUNK
            for g in range(A_GROUPS):
                c0 = g * LANES
                s = _dot(ws_ref[g], vn[r0:r0 + CHUNK, c0:c0 + LANES]) + bs_ref[g]
                gated_ref[base + r0:base + r0 + CHUNK, c0:c0 + LANES] = (
                    u2[r0:r0 + CHUNK, c0:c0 + LANES] * s).astype(jnp.bfloat16)

    def mix(r):
        rows = pl.ds(r * half, half)
        ya = _dot(gated_ref[rows, :], wa_ref[...])
        ya_ref[rows, :] = (jax.nn.sigmoid(st[r]["ga"]) * ya).astype(jnp.bfloat16)

    for stage, r in _GMLP_ORDER:
        {"norm": norm, "v": proj_v, "gate": proj_gate, "u": proj_u, "kv": proj_kv,
         "spatial": spatial, "mix": mix}[stage](r)


def _gmlp_kv(layer, x2, ln_g, w_a, w_kv, gk, gv, ws, bs, wa):
    t = x2.shape[0]
    tm = TOKENS_GMLP
    kw = N_KV * HEAD_DIM
    rows = lambda width: pl.BlockSpec((tm, width), lambda i: (i, 0))
    return pl.pallas_call(
        _gmlp_kv_kernel,
        grid=(t // tm,),
        in_specs=[
            rows(D_MODEL),
            _const_spec((1, D_MODEL), layer),
            _const_spec((D_MODEL, 3 * D_MODEL), layer),
            _const_spec((D_MODEL, 2 * kw), layer),
            _const_spec((1, kw), layer),
            _const_spec((1, D_MODEL), layer),
            _const_spec((A_GROUPS, CHUNK, CHUNK), layer),
            _const_spec((A_GROUPS, CHUNK, LANES), layer),
            _const_spec((D_MODEL, D_MODEL), layer),
        ],
        out_specs=[rows(D_MODEL), rows(D_MODEL), rows(2 * kw)],
        out_shape=[
            jax.ShapeDtypeStruct((t, D_MODEL), jnp.bfloat16),
            jax.ShapeDtypeStruct((t, D_MODEL), jnp.bfloat16),
            jax.ShapeDtypeStruct((t, 2 * kw), jnp.bfloat16),
        ],
        scratch_shapes=[pltpu.VMEM((tm, D_MODEL), jnp.bfloat16)],
        compiler_params=pltpu.CompilerParams(
            dimension_semantics=("arbitrary",), vmem_limit_bytes=VMEM_LIMIT_BYTES),
        name="gmlp_kv",
    )(x2, ln_g, w_a, w_kv, gk, gv, ws, bs, wa)


def _attn_ffn_kernel(x_ref, h_ref, ya_ref, kvp_ref, kvc_ref, kvn_ref,
                     w_ref, gq_ref, sink_ref, bias_ref, wb_ref, wo_ref, ln2_ref, w1_ref, w2_ref,
                     o_ref, attn_ref, gate_ref):
    rows = x_ref.shape[0]
    nblk = rows // CHUNK
    i = pl.program_id(1)
    last = pl.num_programs(1) - 1

    h = h_ref[...]

    low = lax.broadcasted_iota(jnp.int32, (rows, LANES), 1) < HEAD_DIM
    q_low, q_high = {}, {}

    def project_q(pp0):
        c0 = pp0 * LANES
        q = _dot(h, w_ref[:, D_MODEL + c0:D_MODEL + c0 + 2 * LANES])
        for pp in (pp0, pp0 + 1):
            t = q[:, (pp - pp0) * LANES:(pp - pp0 + 1) * LANES]
            ssq = _head_pair_sumsq(t, low)
            qn = t * lax.rsqrt(ssq * (1.0 / HEAD_DIM) + EPS) * gq_ref[:, pp * LANES:(pp + 1) * LANES]
            q_low[pp] = jnp.where(low, qn, 0.0).astype(jnp.bfloat16)
            q_high[pp] = jnp.where(low, 0.0, qn).astype(jnp.bfloat16)

    kv_ext = jnp.concatenate([kvp_ref[...], kvc_ref[...], kvn_ref[...]], axis=0)
    k_ext, v_ext = kv_ext[:, :N_KV * HEAD_DIM], kv_ext[:, N_KV * HEAD_DIM:]
    ones = jnp.ones((3 * CHUNK, LANES), jnp.bfloat16)
    low_blk = lax.broadcasted_iota(jnp.int32, (CHUNK, LANES), 1) < HEAD_DIM
    pen_first = jnp.where(i == 0, MASKED_LOGIT, 0.0).astype(jnp.float32)
    pen_last = jnp.where(i == last, MASKED_LOGIT, 0.0).astype(jnp.float32)

    def scores(jj, pp0):
        r0 = jj * CHUNK
        kv_tile = pp0 // (N_HEADS // N_KV)
        lhs = jnp.concatenate(
            [qh[pp][r0:r0 + CHUNK] for pp in (pp0, pp0 + 1) for qh in (q_low, q_high)], axis=0)
        kd = k_ext[r0:r0 + 3 * CHUNK, kv_tile * LANES:(kv_tile + 1) * LANES]
        return lax.dot_general(lhs, kd, (((1,), (1,)), ((), ())),
                               preferred_element_type=jnp.float32)

    def softmax_numerators(jj, pp0, s):
        ps, sink_terms = [], []
        for hh, head in enumerate(_PAIR_HEADS[pp0] + _PAIR_HEADS[pp0 + 1]):
            sb = []
            for c in range(3):
                t = (s[hh * CHUNK:(hh + 1) * CHUNK, c * LANES:(c + 1) * LANES]
                     + bias_ref[head, :, c * LANES:(c + 1) * LANES])
                if c == 0 and jj == 0:
                    t = t + pen_first
                if c == 2 and jj == nblk - 1:
                    t = t + pen_last
                sb.append(t)
            row_max = jnp.max(jnp.maximum(jnp.maximum(sb[0], sb[1]), sb[2]), axis=-1, keepdims=True)
            sink = sink_ref[head]
            m = jnp.maximum(row_max, sink)
            ps.append(jnp.concatenate([jnp.exp2(t - m).astype(jnp.bfloat16) for t in sb], axis=1))
            sink_terms.append(jnp.exp2(sink - m))
        return jnp.concatenate(ps, axis=0), sink_terms

    def weighted_values(jj, pp0, p, sink_terms):
        r0 = jj * CHUNK
        kv_tile = pp0 // (N_HEADS // N_KV)
        vd = jnp.concatenate(
            [v_ext[r0:r0 + 3 * CHUNK, kv_tile * LANES:(kv_tile + 1) * LANES], ones], axis=1)
        r = _dot(p, vd)
        outs = []
        for hh in range(4):
            rh = r[hh * CHUNK:(hh + 1) * CHUNK]
            outs.append(rh[:, :LANES] / (rh[:, LANES:] + sink_terms[hh]))
        for pr in range(2):
            pair = jnp.where(low_blk, outs[2 * pr], outs[2 * pr + 1])
            c0 = (pp0 + pr) * LANES
            attn_ref[r0:r0 + CHUNK, c0:c0 + LANES] = pair.astype(jnp.bfloat16)

    pp0s = list(range(0, N_HEADS // 2, 2))
    for pp0 in pp0s:
        project_q(pp0)
    steps = [(jj, pp0) for jj in range(nblk) for pp0 in pp0s]
    gate_every = len(steps) // GATE_CHUNKS
    gate_cols = D_MODEL // GATE_CHUNKS
    s_next = scores(*steps[0])
    for n, step in enumerate(steps):
        s_cur = s_next
        if n + 1 < len(steps):
            s_next = scores(*steps[n + 1])
        if n % gate_every == gate_every - 1:
            c0 = (n // gate_every) * gate_cols
            gate_ref[:, c0:c0 + gate_cols] = _dot(h, w_ref[:, c0:c0 + gate_cols])
        p, sink_terms = softmax_numerators(*step, s_cur)
        weighted_values(*step, p, sink_terms)

    yb = _dot(attn_ref[...], wb_ref[...])
    mixed = (jax.nn.sigmoid(gate_ref[...]) * yb + ya_ref[...].astype(jnp.float32)).astype(jnp.bfloat16)
    x_mid = x_ref[...] + _dot(mixed, wo_ref[...])
    o_ref[...] = _ffn_rows(x_mid, ln2_ref, w1_ref, w2_ref)


def _attn_ffn(layer, x3, h3, ya3, kv3, w_b, gq, sink, bias, wb, wo, ln2, w1, w2):
    b, s, _ = x3.shape
    tm = TOKENS_MIX
    nblk = tm // CHUNK
    nb = s // CHUNK
    kw = 2 * N_KV * HEAD_DIM
    tile = lambda bi, i: (bi, i, 0)
    prev = lambda bi, i: (bi, jnp.maximum(i * nblk - 1, 0), 0)
    nxt = lambda bi, i: (bi, jnp.minimum((i + 1) * nblk, nb - 1), 0)
    return pl.pallas_call(
        _attn_ffn_kernel,
        grid=(b, s // tm),
        in_specs=[
            pl.BlockSpec((None, tm, D_MODEL), tile),
            pl.BlockSpec((None, tm, D_MODEL), tile),
            pl.BlockSpec((None, tm, D_MODEL), tile),
            pl.BlockSpec((None, CHUNK, kw), prev),
            pl.BlockSpec((None, tm, kw), tile),
            pl.BlockSpec((None, CHUNK, kw), nxt),
            _const_spec((D_MODEL, 2 * D_MODEL), layer),
            _const_spec((1, D_MODEL), layer),
            _const_spec((N_HEADS, 1, LANES), layer),
            _const_spec((N_HEADS, CHUNK, 3 * CHUNK)),
            _const_spec((D_MODEL, D_MODEL), layer),
            _const_spec((D_MODEL, D_MODEL), layer),
            _const_spec((1, D_MODEL), layer),
            _const_spec((D_MODEL, D_FF), layer),
            _const_spec((D_FF, D_MODEL), layer),
        ],
        out_specs=pl.BlockSpec((None, tm, D_MODEL), tile),
        out_shape=jax.ShapeDtypeStruct((b, s, D_MODEL), jnp.float32),
        scratch_shapes=[pltpu.VMEM((tm, D_MODEL), jnp.bfloat16), pltpu.VMEM((tm, D_MODEL), jnp.float32)],
        compiler_params=pltpu.CompilerParams(
            dimension_semantics=("arbitrary", "arbitrary"), vmem_limit_bytes=VMEM_LIMIT_BYTES),
        name="attn_ffn",
    )(x3, h3, ya3, kv3, kv3, kv3, w_b, gq, sink, bias, wb, wo, ln2, w1, w2)


def _ffn_rows(x, ln_ref, w1_ref, w2_ref):
    h = _rmsnorm_rows(x, ln_ref[...]).astype(jnp.bfloat16)
    acc = x
    for c in range(D_FF // FF_CHUNK):
        a = jnp.maximum(_dot(h, w1_ref[:, c * FF_CHUNK:(c + 1) * FF_CHUNK]), 0.0)
        acc = acc + _dot((a * a).astype(jnp.bfloat16), w2_ref[c * FF_CHUNK:(c + 1) * FF_CHUNK, :])
    return acc


def _alibi_bias():
    qi = np.arange(CHUNK)[:, None]
    kj = np.arange(3 * CHUNK)[None, :]
    rel = np.abs(kj - CHUNK - qi).astype(np.float64)
    slopes = np.asarray(_SLOPES, np.float64)[:, None, None] * LOG2E
    bias = np.where(rel <= WINDOW, -slopes * rel, MASKED_LOGIT)
    return jnp.asarray(bias, jnp.float32)


def kernel(x, ln1_g, w_in, a_norm_g, a_w_s, a_b_s, b_q_norm_g, b_k_norm_g, b_sink,
           w_branch_a, w_branch_b, w_out, ln2_g, w_ff1, w_ff2):
    b, s, d = x.shape
    depth = w_in.shape[0]
    assert d == D_MODEL and s % TOKENS_MIX == 0 and (b * s) % TOKENS_GMLP == 0
    bf = jnp.bfloat16
    o_ga, o_gb, o_za, o_q, o_k, o_v = 0, D_MODEL, 2 * D_MODEL, 4 * D_MODEL, 5 * D_MODEL, 5 * D_MODEL + N_KV * HEAD_DIM

    w_a = jnp.concatenate([w_in[:, :, o_ga:o_gb], w_in[:, :, o_za:o_q]], axis=-1).astype(bf)
    order = np.asarray(_HEAD_ORDER)
    w_q = w_in[:, :, o_q:o_k].reshape(depth, D_MODEL, N_HEADS, HEAD_DIM)[:, :, order].reshape(depth, D_MODEL, D_MODEL)
    w_b = jnp.concatenate([w_in[:, :, o_gb:o_za], w_q], axis=-1).astype(bf)
    w_kv = w_in[:, :, o_k:].astype(bf)
    gk = jnp.tile(b_k_norm_g, (1, N_KV))[:, None, :]
    gq = (jnp.tile(b_q_norm_g, (1, N_HEADS)) * (HEAD_DIM ** -0.5 * LOG2E))[:, None, :]
    sink = jnp.broadcast_to((b_sink * LOG2E)[:, :, None, None], (depth, N_HEADS, 1, LANES))
    ws = (0.5 * a_w_s).astype(bf)
    bs = jnp.broadcast_to(0.5 * a_b_s[:, :, :, None], (depth, A_GROUPS, CHUNK, LANES))
    wa, wo = w_branch_a.astype(bf), w_out.astype(bf)
    wb = w_branch_b.reshape(depth, N_HEADS, HEAD_DIM, D_MODEL)[:, order].reshape(depth, D_MODEL, D_MODEL).astype(bf)
    w1, w2 = w_ff1.astype(bf), w_ff2.astype(bf)
    bias = _alibi_bias()
    ln1, ln2, gv = ln1_g[:, None, :], ln2_g[:, None, :], a_norm_g[:, None, :]

    x2 = x.reshape(b * s, d)
    for l in range(depth):
        ya, h, kv = _gmlp_kv(l, x2, ln1, w_a, w_kv, gk, gv, ws, bs, wa)
        x3 = _attn_ffn(l, x2.reshape(b, s, d), h.reshape(b, s, d), ya.reshape(b, s, d),
                       kv.reshape(b, s, -1), w_b, gq, sink, bias, wb, wo, ln2, w1, w2)
        x2 = x3.reshape(b * s, d)
    return x2.reshape(b, s, d)
```

```python
import functools

import jax
import jax.numpy as jnp
import numpy as np
from jax import lax
from jax.experimental import pallas as pl
from jax.experimental.pallas import tpu as pltpu

D_MODEL = 1024
CHUNK = 128
A_GROUPS = 8
HEAD_DIM = 64
N_HEADS = 16
N_KV = 4
GQ = N_HEADS // N_KV
WINDOW = 128
D_FF = 4 * D_MODEL
EPS = 1e-6
LANES = 128
MASKED_LOGIT = -1e30
LOG2E = float(np.log2(np.e))

VMEM_LIMIT_BYTES = 56 * 1024 * 1024

TOKENS_GMLP = 1024
TOKENS_MIX = 512
GATE_CHUNKS = 4
FF_CHUNK = 512
MLP_PIECE_STEPS = (0, 1, 2, 4, 5, 6, 8, 9, 10, 12, 13, 14)

_SLOPES = [float(2.0 ** (-8.0 * (h + 1.0) / N_HEADS)) for h in range(N_HEADS)]

_PAIR_HEADS = [(2 * GQ * (pp // GQ) + pp % GQ, 2 * GQ * (pp // GQ) + GQ + pp % GQ)
               for pp in range(N_HEADS // 2)]
_HEAD_ORDER = [h for pair in _PAIR_HEADS for h in pair]


def _const_spec(shape, layer=None):
    nd = len(shape)
    if layer is None:
        return pl.BlockSpec(shape, lambda *_: (0,) * nd, pipeline_mode=pl.Buffered(1))
    return pl.BlockSpec((None, *shape), lambda *_: (layer,) + (0,) * nd, pipeline_mode=pl.Buffered(1))


def _rmsnorm_rows(x, g, eps=EPS):
    ms = jnp.mean(x * x, axis=-1, keepdims=True)
    return x * lax.rsqrt(ms + eps) * g


def _gelu_twice(x):
    return x + x * lax.erf(x * np.float32(1.0 / np.sqrt(2.0)))


def _dot(a, b):
    return jnp.dot(a, b, preferred_element_type=jnp.float32)


def _head_pair_sumsq(t, low):
    t2 = t * t
    ssq_all = jnp.sum(t2, axis=-1, keepdims=True)
    ssq_low = jnp.sum(jnp.where(low, t2, 0.0), axis=-1, keepdims=True)
    return jnp.where(low, ssq_low, ssq_all - ssq_low)


_GMLP_ORDER = [("norm", 0), ("v", 0), ("gate", 0), ("u", 0), ("kv", 0),
               ("norm", 1), ("v", 1), ("gate", 1), ("u", 1), ("spatial", 0), ("kv", 1), ("mix", 0),
               ("spatial", 1), ("mix", 1)]


def _gmlp_kv_kernel(x_ref, ln_ref, w_ref, wkv_ref, gk_ref, gv_ref, ws_ref, bs_ref, wa_ref,
                    ya_ref, h_ref, kv_ref, gated_ref):
    half = x_ref.shape[0] // 2
    kw = N_KV * HEAD_DIM
    low = lax.broadcasted_iota(jnp.int32, (half, LANES), 1) < HEAD_DIM

    st = [{}, {}]

    def norm(r):
        rows = pl.ds(r * half, half)
        st[r]["h"] = _rmsnorm_rows(x_ref[rows, :], ln_ref[...]).astype(jnp.bfloat16)
        h_ref[rows, :] = st[r]["h"]

    def proj_v(r):
        v2 = _gelu_twice(_dot(st[r]["h"], w_ref[:, 2 * D_MODEL:]))
        st[r]["vn"] = _rmsnorm_rows(v2, gv_ref[...], eps=4.0 * EPS).astype(jnp.bfloat16)

    def proj_gate(r):
        st[r]["ga"] = _dot(st[r]["h"], w_ref[:, :D_MODEL])

    def proj_u(r):
        st[r]["u2"] = _gelu_twice(_dot(st[r]["h"], w_ref[:, D_MODEL:2 * D_MODEL]))

    def proj_kv(r):
        kv = _dot(st[r]["h"], wkv_ref[...])
        parts = []
        for pp in range(kw // LANES):
            t = kv[:, pp * LANES:(pp + 1) * LANES]
            ssq = _head_pair_sumsq(t, low)
            parts.append(t * lax.rsqrt(ssq * (1.0 / HEAD_DIM) + EPS) * gk_ref[:, pp * LANES:(pp + 1) * LANES])
        kv_ref[pl.ds(r * half, half), :] = jnp.concatenate(parts + [kv[:, kw:]], axis=-1).astype(jnp.bfloat16)

    def spatial(r):
        base = r * half
        vn, u2 = st[r]["vn"], st[r]["u2"]
        for n in range(half // CHUNK):
            r0 = n * CHUNK
            for g in range(A_GROUPS):
                c0 = g * LANES
                s = _dot(ws_ref[g], vn[r0:r0 + CHUNK, c0:c0 + LANES]) + bs_ref[g]
                gated_ref[base + r0:base + r0 + CHUNK, c0:c0 + LANES] = (
                    u2[r0:r0 + CHUNK, c0:c0 + LANES] * s).astype(jnp.bfloat16)

    def mix(r):
        rows = pl.ds(r * half, half)
        ya = _dot(gated_ref[rows, :], wa_ref[...])
        ya_ref[rows, :] = (jax.nn.sigmoid(st[r]["ga"]) * ya).astype(jnp.bfloat16)

    for stage, r in _GMLP_ORDER:
        {"norm": norm, "v": proj_v, "gate": proj_gate, "u": proj_u, "kv": proj_kv,
         "spatial": spatial, "mix": mix}[stage](r)


def _gmlp_kv(layer, x2, ln_g, w_a, w_kv, gk, gv, ws, bs, wa):
    t = x2.shape[0]
    tm = TOKENS_GMLP
    kw = N_KV * HEAD_DIM
    rows = lambda width: pl.BlockSpec((tm, width), lambda i: (i, 0))
    return pl.pallas_call(
        _gmlp_kv_kernel,
        grid=(t // tm,),
        in_specs=[
            rows(D_MODEL),
            _const_spec((1, D_MODEL), layer),
            _const_spec((D_MODEL, 3 * D_MODEL), layer),
            _const_spec((D_MODEL, 2 * kw), layer),
            _const_spec((1, kw), layer),
            _const_spec((1, D_MODEL), layer),
            _const_spec((A_GROUPS, CHUNK, CHUNK), layer),
            _const_spec((A_GROUPS, CHUNK, LANES), layer),
            _const_spec((D_MODEL, D_MODEL), layer),
        ],
        out_specs=[rows(D_MODEL), rows(D_MODEL), rows(2 * kw)],
        out_shape=[
            jax.ShapeDtypeStruct((t, D_MODEL), jnp.bfloat16),
            jax.ShapeDtypeStruct((t, D_MODEL), jnp.bfloat16),
            jax.ShapeDtypeStruct((t, 2 * kw), jnp.bfloat16),
        ],
        scratch_shapes=[pltpu.VMEM((tm, D_MODEL), jnp.bfloat16)],
        compiler_params=pltpu.CompilerParams(
            dimension_semantics=("arbitrary",), vmem_limit_bytes=VMEM_LIMIT_BYTES),
        name="gmlp_kv",
    )(x2, ln_g, w_a, w_kv, gk, gv, ws, bs, wa)


def _attn_ffn_kernel(x_ref, h_ref, ya_ref, kvp_ref, kvc_ref, kvn_ref,
                     w_ref, gq_ref, sink_ref, bias_ref, wb_ref, wo_ref, ln2_ref, w1_ref, w2_ref,
                     o_ref, attn_ref, gate_ref, xmid_ref, hmid_ref, *, tiles_per_seq):
    rows = x_ref.shape[0]
    nblk = rows // CHUNK
    step = pl.program_id(0)
    tile_in_seq = lax.rem(jnp.minimum(step, pl.num_programs(0) - 2), tiles_per_seq)

    @pl.when(step == 0)
    def _():
        xmid_ref[...] = jnp.zeros_like(xmid_ref)
        hmid_ref[...] = jnp.zeros_like(hmid_ref)

    h2 = hmid_ref[...]
    mlp = {"acc": None, "act": None}

    def mlp_up(c):
        a = jnp.maximum(_dot(h2, w1_ref[:, c * FF_CHUNK:(c + 1) * FF_CHUNK]), 0.0)
        mlp["act"] = (a * a).astype(jnp.bfloat16)

    def mlp_down(c):
        base = xmid_ref[...] if mlp["acc"] is None else mlp["acc"]
        mlp["acc"] = base + _dot(mlp["act"], w2_ref[c * FF_CHUNK:(c + 1) * FF_CHUNK, :])

    mlp_pieces = [(f, c) for c in range(D_FF // FF_CHUNK) for f in (mlp_up, mlp_down)]

    h = h_ref[...]

    low = lax.broadcasted_iota(jnp.int32, (rows, LANES), 1) < HEAD_DIM
    q_low, q_high = {}, {}

    def project_q(pp0):
        c0 = pp0 * LANES
        q = _dot(h, w_ref[:, D_MODEL + c0:D_MODEL + c0 + 2 * LANES])
        for pp in (pp0, pp0 + 1):
            t = q[:, (pp - pp0) * LANES:(pp - pp0 + 1) * LANES]
            ssq = _head_pair_sumsq(t, low)
            qn = t * lax.rsqrt(ssq * (1.0 / HEAD_DIM) + EPS) * gq_ref[:, pp * LANES:(pp + 1) * LANES]
            q_low[pp] = jnp.where(low, qn, 0.0).astype(jnp.bfloat16)
            q_high[pp] = jnp.where(low, 0.0, qn).astype(jnp.bfloat16)

    kv_ext = jnp.concatenate([kvp_ref[...], kvc_ref[...], kvn_ref[...]], axis=0)
    k_ext, v_ext = kv_ext[:, :N_KV * HEAD_DIM], kv_ext[:, N_KV * HEAD_DIM:]
    ones = jnp.ones((3 * CHUNK, LANES), jnp.bfloat16)
    low_blk = lax.broadcasted_iota(jnp.int32, (CHUNK, LANES), 1) < HEAD_DIM
    pen_first = jnp.where(tile_in_seq == 0, MASKED_LOGIT, 0.0).astype(jnp.float32)
    pen_last = jnp.where(tile_in_seq == tiles_per_seq - 1, MASKED_LOGIT, 0.0).astype(jnp.float32)

    def scores(jj, pp0):
        r0 = jj * CHUNK
        kv_tile = pp0 // (N_HEADS // N_KV)
        lhs = jnp.concatenate(
            [qh[pp][r0:r0 + CHUNK] for pp in (pp0, pp0 + 1) for qh in (q_low, q_high)], axis=0)
        kd = k_ext[r0:r0 + 3 * CHUNK, kv_tile * LANES:(kv_tile + 1) * LANES]
        return lax.dot_general(lhs, kd, (((1,), (1,)), ((), ())),
                               preferred_element_type=jnp.float32)

    def softmax_numerators(jj, pp0, s):
        ps, sink_terms = [], []
        for hh, head in enumerate(_PAIR_HEADS[pp0] + _PAIR_HEADS[pp0 + 1]):
            sb = []
            for c in range(3):
                t = (s[hh * CHUNK:(hh + 1) * CHUNK, c * LANES:(c + 1) * LANES]
                     + bias_ref[head, :, c * LANES:(c + 1) * LANES])
                if c == 0 and jj == 0:
                    t = t + pen_first
                if c == 2 and jj == nblk - 1:
                    t = t + pen_last
                sb.append(t)
            row_max = jnp.max(jnp.maximum(jnp.maximum(sb[0], sb[1]), sb[2]), axis=-1, keepdims=True)
            sink = sink_ref[head]
            m = jnp.maximum(row_max, sink)
            ps.append(jnp.concatenate([jnp.exp2(t - m).astype(jnp.bfloat16) for t in sb], axis=1))
            sink_terms.append(jnp.exp2(sink - m))
        return jnp.concatenate(ps, axis=0), sink_terms

    def weighted_values(jj, pp0, p, sink_terms):
        r0 = jj * CHUNK
        kv_tile = pp0 // (N_HEADS // N_KV)
        vd = jnp.concatenate(
            [v_ext[r0:r0 + 3 * CHUNK, kv_tile * LANES:(kv_tile + 1) * LANES], ones], axis=1)
        r = _dot(p, vd)
        outs = []
        for hh in range(4):
            rh = r[hh * CHUNK:(hh + 1) * CHUNK]
            outs.append(rh[:, :LANES] / (rh[:, LANES:] + sink_terms[hh]))
        for pr in range(2):
            pair = jnp.where(low_blk, outs[2 * pr], outs[2 * pr + 1])
            c0 = (pp0 + pr) * LANES
            attn_ref[r0:r0 + CHUNK, c0:c0 + LANES] = pair.astype(jnp.bfloat16)

    def next_mlp_piece():
        piece, c = mlp_pieces.pop(0)
        piece(c)

    pp0s = list(range(0, N_HEADS // 2, 2))
    for pp0 in pp0s:
        project_q(pp0)
    next_mlp_piece()
    steps = [(jj, pp0) for jj in range(nblk) for pp0 in pp0s]
    gate_every = len(steps) // GATE_CHUNKS
    gate_cols = D_MODEL // GATE_CHUNKS
    s_next = scores(*steps[0])
    for n, att in enumerate(steps):
        s_cur = s_next
        if n + 1 < len(steps):
            s_next = scores(*steps[n + 1])
        if n % gate_every == gate_every - 1:
            c0 = (n // gate_every) * gate_cols
            gate_ref[:, c0:c0 + gate_cols] = _dot(h, w_ref[:, c0:c0 + gate_cols])
        if n in MLP_PIECE_STEPS:
            next_mlp_piece()
        p, sink_terms = softmax_numerators(*att, s_cur)
        weighted_values(*att, p, sink_terms)

    yb = _dot(attn_ref[...], wb_ref[...])
    next_mlp_piece()
    mixed = (jax.nn.sigmoid(gate_ref[...]) * yb + ya_ref[...].astype(jnp.float32)).astype(jnp.bfloat16)
    x_mid = x_ref[...] + _dot(mixed, wo_ref[...])
    next_mlp_piece()
    next_mlp_piece()
    assert not mlp_pieces
    o_ref[...] = mlp["acc"]
    xmid_ref[...] = x_mid
    hmid_ref[...] = _rmsnorm_rows(x_mid, ln2_ref[...]).astype(jnp.bfloat16)


def _attn_ffn(layer, x3, h3, ya3, kv3, w_b, gq, sink, bias, wb, wo, ln2, w1, w2):
    b, s, _ = x3.shape
    tm = TOKENS_MIX
    nblk = tm // CHUNK
    nb = s // CHUNK
    kw = 2 * N_KV * HEAD_DIM
    tiles_per_seq = s // tm
    n_tiles = b * tiles_per_seq

    def attn_tile(t):
        t = jnp.minimum(t, n_tiles - 1)
        return t // tiles_per_seq, t % tiles_per_seq

    def tile(t):
        bi, i = attn_tile(t)
        return bi, i, 0

    def prev(t):
        bi, i = attn_tile(t)
        return bi, jnp.maximum(i * nblk - 1, 0), 0

    def nxt(t):
        bi, i = attn_tile(t)
        return bi, jnp.minimum((i + 1) * nblk, nb - 1), 0

    def out_tile(t):
        t = jnp.maximum(t - 1, 0)
        return t // tiles_per_seq, t % tiles_per_seq, 0

    return pl.pallas_call(
        functools.partial(_attn_ffn_kernel, tiles_per_seq=tiles_per_seq),
        grid=(n_tiles + 1,),
        in_specs=[
            pl.BlockSpec((None, tm, D_MODEL), tile),
            pl.BlockSpec((None, tm, D_MODEL), tile),
            pl.BlockSpec((None, tm, D_MODEL), tile),
            pl.BlockSpec((None, CHUNK, kw), prev),
            pl.BlockSpec((None, tm, kw), tile),
            pl.BlockSpec((None, CHUNK, kw), nxt),
            _const_spec((D_MODEL, 2 * D_MODEL), layer),
            _const_spec((1, D_MODEL), layer),
            _const_spec((N_HEADS, 1, LANES), layer),
            _const_spec((N_HEADS, CHUNK, 3 * CHUNK)),
            _const_spec((D_MODEL, D_MODEL), layer),
            _const_spec((D_MODEL, D_MODEL), layer),
            _const_spec((1, D_MODEL), layer),
            _const_spec((D_MODEL, D_FF), layer),
            _const_spec((D_FF, D_MODEL), layer),
        ],
        out_specs=pl.BlockSpec((None, tm, D_MODEL), out_tile),
        out_shape=jax.ShapeDtypeStruct((b, s, D_MODEL), jnp.float32),
        scratch_shapes=[pltpu.VMEM((tm, D_MODEL), jnp.bfloat16),
                        pltpu.VMEM((tm, D_MODEL), jnp.float32),
                        pltpu.VMEM((tm, D_MODEL), jnp.float32),
                        pltpu.VMEM((tm, D_MODEL), jnp.bfloat16)],
        compiler_params=pltpu.CompilerParams(
            dimension_semantics=("arbitrary",), vmem_limit_bytes=VMEM_LIMIT_BYTES),
        name="attn_ffn",
    )(x3, h3, ya3, kv3, kv3, kv3, w_b, gq, sink, bias, wb, wo, ln2, w1, w2)


def _alibi_bias():
    qi = np.arange(CHUNK)[:, None]
    kj = np.arange(3 * CHUNK)[None, :]
    rel = np.abs(kj - CHUNK - qi).astype(np.float64)
    slopes = np.asarray(_SLOPES, np.float64)[:, None, None] * LOG2E
    bias = np.where(rel <= WINDOW, -slopes * rel, MASKED_LOGIT)
    return jnp.asarray(bias, jnp.float32)


def kernel(x, ln1_g, w_in, a_norm_g, a_w_s, a_b_s, b_q_norm_g, b_k_norm_g, b_sink,
           w_branch_a, w_branch_b, w_out, ln2_g, w_ff1, w_ff2):
    b, s, d = x.shape
    depth = w_in.shape[0]
    assert d == D_MODEL and s % TOKENS_MIX == 0 and (b * s) % TOKENS_GMLP == 0
    bf = jnp.bfloat16
    o_ga, o_gb, o_za, o_q, o_k, o_v = 0, D_MODEL, 2 * D_MODEL, 4 * D_MODEL, 5 * D_MODEL, 5 * D_MODEL + N_KV * HEAD_DIM

    w_a = jnp.concatenate([w_in[:, :, o_ga:o_gb], w_in[:, :, o_za:o_q]], axis=-1).astype(bf)
    order = np.asarray(_HEAD_ORDER)
    w_q = w_in[:, :, o_q:o_k].reshape(depth, D_MODEL, N_HEADS, HEAD_DIM)[:, :, order].reshape(depth, D_MODEL, D_MODEL)
    w_b = jnp.concatenate([w_in[:, :, o_gb:o_za], w_q], axis=-1).astype(bf)
    w_kv = w_in[:, :, o_k:].astype(bf)
    gk = jnp.tile(b_k_norm_g, (1, N_KV))[:, None, :]
    gq = (jnp.tile(b_q_norm_g, (1, N_HEADS)) * (HEAD_DIM ** -0.5 * LOG2E))[:, None, :]
    sink = jnp.broadcast_to((b_sink * LOG2E)[:, :, None, None], (depth, N_HEADS, 1, LANES))
    ws = (0.5 * a_w_s).astype(bf)
    bs = jnp.broadcast_to(0.5 * a_b_s[:, :, :, None], (depth, A_GROUPS, CHUNK, LANES))
    wa, wo = w_branch_a.astype(bf), w_out.astype(bf)
    wb = w_branch_b.reshape(depth, N_HEADS, HEAD_DIM, D_MODEL)[:, order].reshape(depth, D_MODEL, D_MODEL).astype(bf)
    w1, w2 = w_ff1.astype(bf), w_ff2.astype(bf)
    bias = _alibi_bias()
    ln1, ln2, gv = ln1_g[:, None, :], ln2_g[:, None, :], a_norm_g[:, None, :]

    x2 = x.reshape(b * s, d)
    for l in range(depth):
        ya, h, kv = _gmlp_kv(l, x2, ln1, w_a, w_kv, gk, gv, ws, bs, wa)
        x3 = _attn_ffn(l, x2.reshape(b, s, d), h.reshape(b, s, d), ya.reshape(b, s, d),
                       kv.reshape(b, s, -1), w_b, gq, sink, bias, wb, wo, ln2, w1, w2)
        x2 = x3.reshape(b * s, d)
    return x2.reshape(b, s, d)
```

```python
import functools

import jax
import jax.numpy as jnp
import numpy as np
from jax import lax
from jax.experimental import pallas as pl
from jax.experimental.pallas import tpu as pltpu

D_MODEL = 1024
CHUNK = 128
A_GROUPS = 8
HEAD_DIM = 64
N_HEADS = 16
N_KV = 4
GQ = N_HEADS // N_KV
WINDOW = 128
D_FF = 4 * D_MODEL
EPS = 1e-6
LANES = 128
MASKED_LOGIT = -1e30
LOG2E = float(np.log2(np.e))

VMEM_LIMIT_BYTES = 56 * 1024 * 1024

TOKENS_GMLP = 1024
TOKENS_MIX = 512
GATE_CHUNKS = 4
FF_CHUNK = 512
MLP_PIECE_STEPS = (0, 1, 2, 4, 5, 6, 8, 9, 10, 12, 13, 14)

_SLOPES = [float(2.0 ** (-8.0 * (h + 1.0) / N_HEADS)) for h in range(N_HEADS)]

_PAIR_HEADS = [(2 * GQ * (pp // GQ) + pp % GQ, 2 * GQ * (pp // GQ) + GQ + pp % GQ)
               for pp in range(N_HEADS // 2)]
_HEAD_ORDER = [h for pair in _PAIR_HEADS for h in pair]


def _const_spec(shape, layer=None):
    nd = len(shape)
    if layer is None:
        return pl.BlockSpec(shape, lambda *_: (0,) * nd, pipeline_mode=pl.Buffered(1))
    return pl.BlockSpec((None, *shape), lambda *_: (layer,) + (0,) * nd, pipeline_mode=pl.Buffered(1))


def _rmsnorm_rows(x, g, eps=EPS):
    ms = jnp.mean(x * x, axis=-1, keepdims=True)
    return x * lax.rsqrt(ms + eps) * g


def _gelu_twice(x):
    return x + x * lax.erf(x * np.float32(1.0 / np.sqrt(2.0)))


def _dot(a, b):
    return jnp.dot(a, b, preferred_element_type=jnp.float32)


def _head_pair_sumsq(t, low):
    t2 = t * t
    ssq_all = jnp.sum(t2, axis=-1, keepdims=True)
    ssq_low = jnp.sum(jnp.where(low, t2, 0.0), axis=-1, keepdims=True)
    return jnp.where(low, ssq_low, ssq_all - ssq_low)


_GMLP_ORDER = [("norm", 0), ("v", 0), ("gate", 0), ("u", 0), ("kv", 0),
               ("norm", 1), ("v", 1), ("gate", 1), ("u", 1), ("spatial", 0), ("kv", 1), ("mix", 0),
               ("spatial", 1), ("mix", 1)]


def _gmlp_kv_kernel(x_ref, ln_ref, w_ref, wkv_ref, gk_ref, gv_ref, ws_ref, bs_ref, wa_ref,
                    ya_ref, h_ref, kv_ref, gated_ref):
    half = x_ref.shape[0] // 2
    kw = N_KV * HEAD_DIM
    low = lax.broadcasted_iota(jnp.int32, (half, LANES), 1) < HEAD_DIM

    st = [{}, {}]

    def norm(r):
        rows = pl.ds(r * half, half)
        st[r]["h"] = _rmsnorm_rows(x_ref[rows, :], ln_ref[...]).astype(jnp.bfloat16)
        h_ref[rows, :] = st[r]["h"]

    def proj_v(r):
        v2 = _gelu_twice(_dot(st[r]["h"], w_ref[:, 2 * D_MODEL:]))
        st[r]["vn"] = _rmsnorm_rows(v2, gv_ref[...], eps=4.0 * EPS).astype(jnp.bfloat16)

    def proj_gate(r):
        st[r]["ga"] = _dot(st[r]["h"], w_ref[:, :D_MODEL])

    def proj_u(r):
        st[r]["u2"] = _gelu_twice(_dot(st[r]["h"], w_ref[:, D_MODEL:2 * D_MODEL]))

    def proj_kv(r):
        kv = _dot(st[r]["h"], wkv_ref[...])
        parts = []
        for pp in range(kw // LANES):
            t = kv[:, pp * LANES:(pp + 1) * LANES]
            ssq = _head_pair_sumsq(t, low)
            parts.append(t * lax.rsqrt(ssq * (1.0 / HEAD_DIM) + EPS) * gk_ref[:, pp * LANES:(pp + 1) * LANES])
        kv_ref[pl.ds(r * half, half), :] = jnp.concatenate(parts + [kv[:, kw:]], axis=-1).astype(jnp.bfloat16)

    def spatial(r):
        base = r * half
        vn, u2 = st[r]["vn"], st[r]["u2"]
        for n in range(half // CHUNK):
            r0 = n * CHUNK
            for g in range(A_GROUPS):
                c0 = g * LANES
                s = _dot(ws_ref[g], vn[r0:r0 + CHUNK, c0:c0 + LANES]) + bs_ref[g]
                gated_ref[base + r0:base + r0 + CHUNK, c0:c0 + LANES] = (
                    u2[r0:r0 + CHUNK, c0:c0 + LANES] * s).astype(jnp.bfloat16)

    def mix(r):
        rows = pl.ds(r * half, half)
        ya = _dot(gated_ref[rows, :], wa_ref[...])
        ya_ref[rows, :] = (jax.nn.sigmoid(st[r]["ga"]) * ya).astype(jnp.bfloat16)

    for stage, r in _GMLP_ORDER:
        {"norm": norm, "v": proj_v, "gate": proj_gate, "u": proj_u, "kv": proj_kv,
         "spatial": spatial, "mix": mix}[stage](r)


def _gmlp_kv(layer, x2, ln_g, w_a, w_kv, gk, gv, ws, bs, wa):
    t = x2.shape[0]
    tm = TOKENS_GMLP
    kw = N_KV * HEAD_DIM
    rows = lambda width: pl.BlockSpec((tm, width), lambda i: (i, 0))
    return pl.pallas_call(
        _gmlp_kv_kernel,
        grid=(t // tm,),
        in_specs=[
            rows(D_MODEL),
            _const_spec((1, D_MODEL), layer),
            _const_spec((D_MODEL, 3 * D_MODEL), layer),
            _const_spec((D_MODEL, 2 * kw), layer),
            _const_spec((1, kw), layer),
            _const_spec((1, D_MODEL), layer),
            _const_spec((A_GROUPS, CHUNK, CHUNK), layer),
            _const_spec((A_GROUPS, CHUNK, LANES), layer),
            _const_spec((D_MODEL, D_MODEL), layer),
        ],
        out_specs=[rows(D_MODEL), rows(D_MODEL), rows(2 * kw)],
        out_shape=[
            jax.ShapeDtypeStruct((t, D_MODEL), jnp.bfloat16),
            jax.ShapeDtypeStruct((t, D_MODEL), jnp.bfloat16),
            jax.ShapeDtypeStruct((t, 2 * kw), jnp.bfloat16),
        ],
        scratch_shapes=[pltpu.VMEM((tm, D_MODEL), jnp.bfloat16)],
        compiler_params=pltpu.CompilerParams(
            dimension_semantics=("arbitrary",), vmem_limit_bytes=VMEM_LIMIT_BYTES),
        name="gmlp_kv",
    )(x2, ln_g, w_a, w_kv, gk, gv, ws, bs, wa)


def _attn_ffn_kernel(x_ref, h_ref, ya_ref, kvp_ref, kvc_ref, kvn_ref,
                     w_ref, gq_ref, sink_ref, bias_ref, wb_ref, wo_ref, ln2_ref, w1_ref, w2_ref,
                     o_ref, attn_ref, gate_ref, xmid_ref, hmid_ref, *, tiles_per_seq):
    rows = x_ref.shape[0]
    nblk = rows // CHUNK
    step = pl.program_id(0)
    tile_in_seq = lax.rem(jnp.minimum(step, pl.num_programs(0) - 2), tiles_per_seq)

    @pl.when(step == 0)
    def _():
        xmid_ref[...] = jnp.zeros_like(xmid_ref)
        hmid_ref[...] = jnp.zeros_like(hmid_ref)

    mlp = {"act": None}

    def mlp_up(c):
        a = jnp.maximum(_dot(hmid_ref[...], w1_ref[:, c * FF_CHUNK:(c + 1) * FF_CHUNK]), 0.0)
        mlp["act"] = (a * a).astype(jnp.bfloat16)

    def mlp_down(c):
        base = xmid_ref[...] if c == 0 else o_ref[...]
        o_ref[...] = base + _dot(mlp["act"], w2_ref[c * FF_CHUNK:(c + 1) * FF_CHUNK, :])

    mlp_pieces = [(f, c) for c in range(D_FF // FF_CHUNK) for f in (mlp_up, mlp_down)]

    low = lax.broadcasted_iota(jnp.int32, (rows, LANES), 1) < HEAD_DIM
    q_low, q_high = {}, {}

    def project_q(pp0):
        c0 = pp0 * LANES
        q = _dot(h_ref[...], w_ref[:, D_MODEL + c0:D_MODEL + c0 + 2 * LANES])
        for pp in (pp0, pp0 + 1):
            t = q[:, (pp - pp0) * LANES:(pp - pp0 + 1) * LANES]
            ssq = _head_pair_sumsq(t, low)
            qn = t * lax.rsqrt(ssq * (1.0 / HEAD_DIM) + EPS) * gq_ref[:, pp * LANES:(pp + 1) * LANES]
            q_low[pp] = jnp.where(low, qn, 0.0).astype(jnp.bfloat16)
            q_high[pp] = jnp.where(low, 0.0, qn).astype(jnp.bfloat16)

    kv_ext = jnp.concatenate([kvp_ref[...], kvc_ref[...], kvn_ref[...]], axis=0)
    k_ext, v_ext = kv_ext[:, :N_KV * HEAD_DIM], kv_ext[:, N_KV * HEAD_DIM:]
    ones = jnp.ones((3 * CHUNK, LANES), jnp.bfloat16)
    low_blk = lax.broadcasted_iota(jnp.int32, (CHUNK, LANES), 1) < HEAD_DIM
    pen_first = jnp.where(tile_in_seq == 0, MASKED_LOGIT, 0.0).astype(jnp.float32)
    pen_last = jnp.where(tile_in_seq == tiles_per_seq - 1, MASKED_LOGIT, 0.0).astype(jnp.float32)

    def scores(jj, pp0):
        r0 = jj * CHUNK
        kv_tile = pp0 // (N_HEADS // N_KV)
        lhs = jnp.concatenate(
            [qh[pp][r0:r0 + CHUNK] for pp in (pp0, pp0 + 1) for qh in (q_low, q_high)], axis=0)
        kd = k_ext[r0:r0 + 3 * CHUNK, kv_tile * LANES:(kv_tile + 1) * LANES]
        return lax.dot_general(lhs, kd, (((1,), (1,)), ((), ())),
                               preferred_element_type=jnp.float32)

    def softmax_numerators(jj, pp0, s):
        ps, sink_terms = [], []
        for hh, head in enumerate(_PAIR_HEADS[pp0] + _PAIR_HEADS[pp0 + 1]):
            sb = []
            for c in range(3):
                t = (s[hh * CHUNK:(hh + 1) * CHUNK, c * LANES:(c + 1) * LANES]
                     + bias_ref[head, :, c * LANES:(c + 1) * LANES])
                if c == 0 and jj == 0:
                    t = t + pen_first
                if c == 2 and jj == nblk - 1:
                    t = t + pen_last
                sb.append(t)
            row_max = jnp.max(jnp.maximum(jnp.maximum(sb[0], sb[1]), sb[2]), axis=-1, keepdims=True)
            sink = sink_ref[head]
            m = jnp.maximum(row_max, sink)
            ps.append(jnp.concatenate([jnp.exp2(t - m).astype(jnp.bfloat16) for t in sb], axis=1))
            sink_terms.append(jnp.exp2(sink - m))
        return jnp.concatenate(ps, axis=0), sink_terms

    def weighted_values(jj, pp0, p, sink_terms):
        r0 = jj * CHUNK
        kv_tile = pp0 // (N_HEADS // N_KV)
        vd = jnp.concatenate(
            [v_ext[r0:r0 + 3 * CHUNK, kv_tile * LANES:(kv_tile + 1) * LANES], ones], axis=1)
        r = _dot(p, vd)
        outs = []
        for hh in range(4):
            rh = r[hh * CHUNK:(hh + 1) * CHUNK]
            outs.append(rh[:, :LANES] / (rh[:, LANES:] + sink_terms[hh]))
        for pr in range(2):
            pair = jnp.where(low_blk, outs[2 * pr], outs[2 * pr + 1])
            c0 = (pp0 + pr) * LANES
            attn_ref[r0:r0 + CHUNK, c0:c0 + LANES] = pair.astype(jnp.bfloat16)

    def next_mlp_piece():
        piece, c = mlp_pieces.pop(0)
        piece(c)

    pp0s = list(range(0, N_HEADS // 2, 2))
    for pp0 in pp0s:
        project_q(pp0)
    next_mlp_piece()
    steps = [(jj, pp0) for jj in range(nblk) for pp0 in pp0s]
    gate_every = len(steps) // GATE_CHUNKS
    gate_cols = D_MODEL // GATE_CHUNKS
    s_next = scores(*steps[0])
    for n, att in enumerate(steps):
        s_cur = s_next
        if n + 1 < len(steps):
            s_next = scores(*steps[n + 1])
        if n % gate_every == gate_every - 1:
            c0 = (n // gate_every) * gate_cols
            gate_ref[:, c0:c0 + gate_cols] = _dot(h_ref[...], w_ref[:, c0:c0 + gate_cols])
        if n in MLP_PIECE_STEPS:
            next_mlp_piece()
        p, sink_terms = softmax_numerators(*att, s_cur)
        weighted_values(*att, p, sink_terms)

    yb = _dot(attn_ref[...], wb_ref[...])
    next_mlp_piece()
    mixed = (jax.nn.sigmoid(gate_ref[...]) * yb + ya_ref[...].astype(jnp.float32)).astype(jnp.bfloat16)
    x_mid = x_ref[...] + _dot(mixed, wo_ref[...])
    next_mlp_piece()
    next_mlp_piece()
    assert not mlp_pieces
    xmid_ref[...] = x_mid
    hmid_ref[...] = _rmsnorm_rows(x_mid, ln2_ref[...]).astype(jnp.bfloat16)


def _attn_ffn(layer, x3, h3, ya3, kv3, w_b, gq, sink, bias, wb, wo, ln2, w1, w2):
    b, s, _ = x3.shape
    tm = TOKENS_MIX
    nblk = tm // CHUNK
    nb = s // CHUNK
    kw = 2 * N_KV * HEAD_DIM
    tiles_per_seq = s // tm
    n_tiles = b * tiles_per_seq

    def attn_tile(t):
        t = jnp.minimum(t, n_tiles - 1)
        return t // tiles_per_seq, t % tiles_per_seq

    def tile(t):
        bi, i = attn_tile(t)
        return bi, i, 0

    def prev(t):
        bi, i = attn_tile(t)
        return bi, jnp.maximum(i * nblk - 1, 0), 0

    def nxt(t):
        bi, i = attn_tile(t)
        return bi, jnp.minimum((i + 1) * nblk, nb - 1), 0

    def out_tile(t):
        t = jnp.maximum(t - 1, 0)
        return t // tiles_per_seq, t % tiles_per_seq, 0

    return pl.pallas_call(
        functools.partial(_attn_ffn_kernel, tiles_per_seq=tiles_per_seq),
        grid=(n_tiles + 1,),
        in_specs=[
            pl.BlockSpec((None, tm, D_MODEL), tile),
            pl.BlockSpec((None, tm, D_MODEL), tile),
            pl.BlockSpec((None, tm, D_MODEL), tile),
            pl.BlockSpec((None, CHUNK, kw), prev),
            pl.BlockSpec((None, tm, kw), tile),
            pl.BlockSpec((None, CHUNK, kw), nxt),
            _const_spec((D_MODEL, 2 * D_MODEL), layer),
            _const_spec((1, D_MODEL), layer),
            _const_spec((N_HEADS, 1, LANES), layer),
            _const_spec((N_HEADS, CHUNK, 3 * CHUNK)),
            _const_spec((D_MODEL, D_MODEL), layer),
            _const_spec((D_MODEL, D_MODEL), layer),
            _const_spec((1, D_MODEL), layer),
            _const_spec((D_MODEL, D_FF), layer),
            _const_spec((D_FF, D_MODEL), layer),
        ],
        out_specs=pl.BlockSpec((None, tm, D_MODEL), out_tile),
        out_shape=jax.ShapeDtypeStruct((b, s, D_MODEL), jnp.float32),
        scratch_shapes=[pltpu.VMEM((tm, D_MODEL), jnp.bfloat16),
                        pltpu.VMEM((tm, D_MODEL), jnp.float32),
                        pltpu.VMEM((tm, D_MODEL), jnp.float32),
                        pltpu.VMEM((tm, D_MODEL), jnp.bfloat16)],
        compiler_params=pltpu.CompilerParams(
            dimension_semantics=("arbitrary",), vmem_limit_bytes=VMEM_LIMIT_BYTES),
        name="attn_ffn",
    )(x3, h3, ya3, kv3, kv3, kv3, w_b, gq, sink, bias, wb, wo, ln2, w1, w2)


def _alibi_bias():
    qi = np.arange(CHUNK)[:, None]
    kj = np.arange(3 * CHUNK)[None, :]
    rel = np.abs(kj - CHUNK - qi).astype(np.float64)
    slopes = np.asarray(_SLOPES, np.float64)[:, None, None] * LOG2E
    bias = np.where(rel <= WINDOW, -slopes * rel, MASKED_LOGIT)
    return jnp.asarray(bias, jnp.float32)


def kernel(x, ln1_g, w_in, a_norm_g, a_w_s, a_b_s, b_q_norm_g, b_k_norm_g, b_sink,
           w_branch_a, w_branch_b, w_out, ln2_g, w_ff1, w_ff2):
    b, s, d = x.shape
    depth = w_in.shape[0]
    assert d == D_MODEL and s % TOKENS_MIX == 0 and (b * s) % TOKENS_GMLP == 0
    bf = jnp.bfloat16
    o_ga, o_gb, o_za, o_q, o_k, o_v = 0, D_MODEL, 2 * D_MODEL, 4 * D_MODEL, 5 * D_MODEL, 5 * D_MODEL + N_KV * HEAD_DIM

    w_a = jnp.concatenate([w_in[:, :, o_ga:o_gb], w_in[:, :, o_za:o_q]], axis=-1).astype(bf)
    src = np.asarray(_HEAD_ORDER)[np.arange(D_MODEL) // HEAD_DIM] * HEAD_DIM + np.arange(D_MODEL) % HEAD_DIM
    perm = jnp.asarray(np.arange(D_MODEL)[:, None] == src[None, :], bf)
    w_q = jnp.einsum("ldk,kn->ldn", w_in[:, :, o_q:o_k].astype(bf), perm, preferred_element_type=bf)
    w_b = jnp.concatenate([w_in[:, :, o_gb:o_za].astype(bf), w_q], axis=-1)
    w_kv = w_in[:, :, o_k:].astype(bf)
    gk = jnp.tile(b_k_norm_g, (1, N_KV))[:, None, :]
    gq = (jnp.tile(b_q_norm_g, (1, N_HEADS)) * (HEAD_DIM ** -0.5 * LOG2E))[:, None, :]
    sink = jnp.broadcast_to((b_sink * LOG2E)[:, :, None, None], (depth, N_HEADS, 1, LANES))
    ws = (0.5 * a_w_s).astype(bf)
    bs = jnp.broadcast_to(0.5 * a_b_s[:, :, :, None], (depth, A_GROUPS, CHUNK, LANES))
    wa, wo = w_branch_a.astype(bf), w_out.astype(bf)
    wb = jnp.einsum("kn,lkd->lnd", perm, w_branch_b.astype(bf), preferred_element_type=bf)
    w1, w2 = w_ff1.astype(bf), w_ff2.astype(bf)
    bias = _alibi_bias()
    ln1, ln2, gv = ln1_g[:, None, :], ln2_g[:, None, :], a_norm_g[:, None, :]

    x2 = x.reshape(b * s, d)
    for l in range(depth):
        ya, h, kv = _gmlp_kv(l, x2, ln1, w_a, w_kv, gk, gv, ws, bs, wa)
        x3 = _attn_ffn(l, x2.reshape(b, s, d), h.reshape(b, s, d), ya.reshape(b, s, d),
                       kv.reshape(b, s, -1), w_b, gq, sink, bias, wb, wo, ln2, w1, w2)
        x2 = x3.reshape(b * s, d)
    return x2.reshape(b, s, d)
```

```python
import functools

import jax
import jax.numpy as jnp
import numpy as np
from jax import lax
from jax.experimental import pallas as pl
from jax.experimental.pallas import tpu as pltpu

D_MODEL = 1024
CHUNK = 128
A_GROUPS = 8
HEAD_DIM = 64
N_HEADS = 16
N_KV = 4
GQ = N_HEADS // N_KV
WINDOW = 128
D_FF = 4 * D_MODEL
EPS = 1e-6
LANES = 128
MASKED_LOGIT = -1e30
LOG2E = float(np.log2(np.e))

VMEM_LIMIT_BYTES = 56 * 1024 * 1024

TOKENS_GMLP = 1024
TOKENS_MIX = 512
GATE_CHUNKS = 4
FF_CHUNK = 512
MLP_PIECE_STEPS = (0, 1, 2, 4, 5, 6, 8, 9, 10, 12, 13, 14)

_SLOPES = [float(2.0 ** (-8.0 * (h + 1.0) / N_HEADS)) for h in range(N_HEADS)]

_PAIR_HEADS = [(2 * GQ * (pp // GQ) + pp % GQ, 2 * GQ * (pp // GQ) + GQ + pp % GQ)
               for pp in range(N_HEADS // 2)]
_HEAD_ORDER = [h for pair in _PAIR_HEADS for h in pair]


def _const_spec(shape, layer=None):
    nd = len(shape)
    if layer is None:
        return pl.BlockSpec(shape, lambda *_: (0,) * nd, pipeline_mode=pl.Buffered(1))
    return pl.BlockSpec((None, *shape), lambda *_: (layer,) + (0,) * nd, pipeline_mode=pl.Buffered(1))


def _rmsnorm_rows(x, g, eps=EPS):
    ms = jnp.mean(x * x, axis=-1, keepdims=True)
    return x * lax.rsqrt(ms + eps) * g


def _gelu_twice(x):
    return x + x * lax.erf(x * np.float32(1.0 / np.sqrt(2.0)))


def _dot(a, b):
    return jnp.dot(a, b, preferred_element_type=jnp.float32)


def _head_pair_sumsq(t, low):
    t2 = t * t
    ssq_all = jnp.sum(t2, axis=-1, keepdims=True)
    ssq_low = jnp.sum(jnp.where(low, t2, 0.0), axis=-1, keepdims=True)
    return jnp.where(low, ssq_low, ssq_all - ssq_low)


_GMLP_ORDER = [("norm", 0), ("v", 0), ("gate", 0), ("u", 0), ("kv", 0),
               ("norm", 1), ("v", 1), ("gate", 1), ("u", 1), ("spatial", 0), ("kv", 1), ("mix", 0),
               ("spatial", 1), ("mix", 1)]


def _gmlp_kv_kernel(x_ref, ln_ref, w_ref, wkv_ref, gk_ref, gv_ref, ws_ref, bs_ref, wa_ref,
                    ya_ref, h_ref, kv_ref, gated_ref):
    half = x_ref.shape[0] // 2
    kw = N_KV * HEAD_DIM
    low = lax.broadcasted_iota(jnp.int32, (half, LANES), 1) < HEAD_DIM

    st = [{}, {}]

    def norm(r):
        rows = pl.ds(r * half, half)
        st[r]["h"] = _rmsnorm_rows(x_ref[rows, :], ln_ref[...]).astype(jnp.bfloat16)
        h_ref[rows, :] = st[r]["h"]

    def proj_v(r):
        v2 = _gelu_twice(_dot(st[r]["h"], w_ref[:, 2 * D_MODEL:]))
        st[r]["vn"] = _rmsnorm_rows(v2, gv_ref[...], eps=4.0 * EPS).astype(jnp.bfloat16)

    def proj_gate(r):
        st[r]["ga"] = _dot(st[r]["h"], w_ref[:, :D_MODEL])

    def proj_u(r):
        st[r]["u2"] = _gelu_twice(_dot(st[r]["h"], w_ref[:, D_MODEL:2 * D_MODEL]))

    def proj_kv(r):
        kv = _dot(st[r]["h"], wkv_ref[...])
        parts = []
        for pp in range(kw // LANES):
            t = kv[:, pp * LANES:(pp + 1) * LANES]
            ssq = _head_pair_sumsq(t, low)
            parts.append(t * lax.rsqrt(ssq * (1.0 / HEAD_DIM) + EPS) * gk_ref[:, pp * LANES:(pp + 1) * LANES])
        kv_ref[pl.ds(r * half, half), :] = jnp.concatenate(parts + [kv[:, kw:]], axis=-1).astype(jnp.bfloat16)

    def spatial(r):
        base = r * half
        vn, u2 = st[r]["vn"], st[r]["u2"]
        for n in range(half // CHUNK):
            r0 = n * CHUNK
            for g in range(A_GROUPS):
                c0 = g * LANES
                s = _dot(ws_ref[g], vn[r0:r0 + CHUNK, c0:c0 + LANES]) + bs_ref[g]
                gated_ref[base + r0:base + r0 + CHUNK, c0:c0 + LANES] = (
                    u2[r0:r0 + CHUNK, c0:c0 + LANES] * s).astype(jnp.bfloat16)

    def mix(r):
        rows = pl.ds(r * half, half)
        ya = _dot(gated_ref[rows, :], wa_ref[...])
        ya_ref[rows, :] = (jax.nn.sigmoid(st[r]["ga"]) * ya).astype(jnp.bfloat16)

    for stage, r in _GMLP_ORDER:
        {"norm": norm, "v": proj_v, "gate": proj_gate, "u": proj_u, "kv": proj_kv,
         "spatial": spatial, "mix": mix}[stage](r)


def _gmlp_kv(layer, x2, ln_g, w_a, w_kv, gk, gv, ws, bs, wa):
    t = x2.shape[0]
    tm = TOKENS_GMLP
    kw = N_KV * HEAD_DIM
    rows = lambda width: pl.BlockSpec((tm, width), lambda i: (i, 0))
    return pl.pallas_call(
        _gmlp_kv_kernel,
        grid=(t // tm,),
        in_specs=[
            rows(D_MODEL),
            _const_spec((1, D_MODEL), layer),
            _const_spec((D_MODEL, 3 * D_MODEL), layer),
            _const_spec((D_MODEL, 2 * kw), layer),
            _const_spec((1, kw), layer),
            _const_spec((1, D_MODEL), layer),
            _const_spec((A_GROUPS, CHUNK, CHUNK), layer),
            _const_spec((A_GROUPS, CHUNK, LANES), layer),
            _const_spec((D_MODEL, D_MODEL), layer),
        ],
        out_specs=[rows(D_MODEL), rows(D_MODEL), rows(2 * kw)],
        out_shape=[
            jax.ShapeDtypeStruct((t, D_MODEL), jnp.bfloat16),
            jax.ShapeDtypeStruct((t, D_MODEL), jnp.bfloat16),
            jax.ShapeDtypeStruct((t, 2 * kw), jnp.bfloat16),
        ],
        scratch_shapes=[pltpu.VMEM((tm, D_MODEL), jnp.bfloat16)],
        compiler_params=pltpu.CompilerParams(
            dimension_semantics=("arbitrary",), vmem_limit_bytes=VMEM_LIMIT_BYTES),
        name="gmlp_kv",
    )(x2, ln_g, w_a, w_kv, gk, gv, ws, bs, wa)


def _attn_ffn_kernel(x_ref, h_ref, ya_ref, kvp_ref, kvc_ref, kvn_ref,
                     w_ref, gq_ref, sink_ref, bias_ref, wb_ref, wo_ref, ln2_ref, w1_ref, w2_ref,
                     o_ref, attn_ref, gate_ref, xmid_ref, hmid_ref, *, tiles_per_seq):
    rows = x_ref.shape[0]
    nblk = rows // CHUNK
    step = pl.program_id(0)
    tile_in_seq = lax.rem(jnp.minimum(step, pl.num_programs(0) - 2), tiles_per_seq)

    @pl.when(step == 0)
    def _():
        xmid_ref[...] = jnp.zeros_like(xmid_ref)

    hmid_ref[...] = _rmsnorm_rows(xmid_ref[...], ln2_ref[...]).astype(jnp.bfloat16)

    mlp = {"act": None}

    def mlp_up(c):
        a = jnp.maximum(_dot(hmid_ref[...], w1_ref[:, c * FF_CHUNK:(c + 1) * FF_CHUNK]), 0.0)
        mlp["act"] = (a * a).astype(jnp.bfloat16)

    def mlp_down(c):
        base = xmid_ref[...] if c == 0 else o_ref[...]
        o_ref[...] = base + _dot(mlp["act"], w2_ref[c * FF_CHUNK:(c + 1) * FF_CHUNK, :])

    mlp_pieces = [(f, c) for c in range(D_FF // FF_CHUNK) for f in (mlp_up, mlp_down)]

    low = lax.broadcasted_iota(jnp.int32, (rows, LANES), 1) < HEAD_DIM
    q_low, q_high = {}, {}

    def project_q(pp0):
        c0 = pp0 * LANES
        q = _dot(h_ref[...], w_ref[:, D_MODEL + c0:D_MODEL + c0 + 2 * LANES])
        for pp in (pp0, pp0 + 1):
            t = q[:, (pp - pp0) * LANES:(pp - pp0 + 1) * LANES]
            ssq = _head_pair_sumsq(t, low)
            qn = t * lax.rsqrt(ssq * (1.0 / HEAD_DIM) + EPS) * gq_ref[:, pp * LANES:(pp + 1) * LANES]
            q_low[pp] = jnp.where(low, qn, 0.0).astype(jnp.bfloat16)
            q_high[pp] = jnp.where(low, 0.0, qn).astype(jnp.bfloat16)

    kv_ext = jnp.concatenate([kvp_ref[...], kvc_ref[...], kvn_ref[...]], axis=0)
    k_ext, v_ext = kv_ext[:, :N_KV * HEAD_DIM], kv_ext[:, N_KV * HEAD_DIM:]
    ones = jnp.ones((3 * CHUNK, LANES), jnp.bfloat16)
    low_blk = lax.broadcasted_iota(jnp.int32, (CHUNK, LANES), 1) < HEAD_DIM
    pen_first = jnp.where(tile_in_seq == 0, MASKED_LOGIT, 0.0).astype(jnp.float32)
    pen_last = jnp.where(tile_in_seq == tiles_per_seq - 1, MASKED_LOGIT, 0.0).astype(jnp.float32)

    def scores(jj, pp0):
        r0 = jj * CHUNK
        kv_tile = pp0 // (N_HEADS // N_KV)
        lhs = jnp.concatenate(
            [qh[pp][r0:r0 + CHUNK] for pp in (pp0, pp0 + 1) for qh in (q_low, q_high)], axis=0)
        kd = k_ext[r0:r0 + 3 * CHUNK, kv_tile * LANES:(kv_tile + 1) * LANES]
        return lax.dot_general(lhs, kd, (((1,), (1,)), ((), ())),
                               preferred_element_type=jnp.float32)

    def softmax_numerators(jj, pp0, s):
        ps, sink_terms = [], []
        for hh, head in enumerate(_PAIR_HEADS[pp0] + _PAIR_HEADS[pp0 + 1]):
            sb = []
            for c in range(3):
                t = (s[hh * CHUNK:(hh + 1) * CHUNK, c * LANES:(c + 1) * LANES]
                     + bias_ref[head, :, c * LANES:(c + 1) * LANES])
                if c == 0 and jj == 0:
                    t = t + pen_first
                if c == 2 and jj == nblk - 1:
                    t = t + pen_last
                sb.append(t)
            row_max = jnp.max(jnp.maximum(jnp.maximum(sb[0], sb[1]), sb[2]), axis=-1, keepdims=True)
            sink = sink_ref[head]
            m = jnp.maximum(row_max, sink)
            ps.append(jnp.concatenate([jnp.exp2(t - m).astype(jnp.bfloat16) for t in sb], axis=1))
            sink_terms.append(jnp.exp2(sink - m))
        return jnp.concatenate(ps, axis=0), sink_terms

    def weighted_values(jj, pp0, p, sink_terms):
        r0 = jj * CHUNK
        kv_tile = pp0 // (N_HEADS // N_KV)
        vd = jnp.concatenate(
            [v_ext[r0:r0 + 3 * CHUNK, kv_tile * LANES:(kv_tile + 1) * LANES], ones], axis=1)
        r = _dot(p, vd)
        outs = []
        for hh in range(4):
            rh = r[hh * CHUNK:(hh + 1) * CHUNK]
            outs.append(rh[:, :LANES] / (rh[:, LANES:] + sink_terms[hh]))
        for pr in range(2):
            pair = jnp.where(low_blk, outs[2 * pr], outs[2 * pr + 1])
            c0 = (pp0 + pr) * LANES
            attn_ref[r0:r0 + CHUNK, c0:c0 + LANES] = pair.astype(jnp.bfloat16)

    def next_mlp_piece():
        piece, c = mlp_pieces.pop(0)
        piece(c)

    pp0s = list(range(0, N_HEADS // 2, 2))
    for pp0 in pp0s:
        project_q(pp0)
    next_mlp_piece()
    steps = [(jj, pp0) for jj in range(nblk) for pp0 in pp0s]
    gate_every = len(steps) // GATE_CHUNKS
    gate_cols = D_MODEL // GATE_CHUNKS
    s_next = scores(*steps[0])
    for n, att in enumerate(steps):
        s_cur = s_next
        if n + 1 < len(steps):
            s_next = scores(*steps[n + 1])
        if n % gate_every == gate_every - 1:
            c0 = (n // gate_every) * gate_cols
            gate_ref[:, c0:c0 + gate_cols] = _dot(h_ref[...], w_ref[:, c0:c0 + gate_cols])
        if n in MLP_PIECE_STEPS:
            next_mlp_piece()
        p, sink_terms = softmax_numerators(*att, s_cur)
        weighted_values(*att, p, sink_terms)

    yb = _dot(attn_ref[...], wb_ref[...])
    next_mlp_piece()
    mixed = (jax.nn.sigmoid(gate_ref[...]) * yb + ya_ref[...].astype(jnp.float32)).astype(jnp.bfloat16)
    x_mid = x_ref[...] + _dot(mixed, wo_ref[...])
    next_mlp_piece()
    next_mlp_piece()
    assert not mlp_pieces
    xmid_ref[...] = x_mid


def _attn_ffn(layer, x3, h3, ya3, kv3, w_b, gq, sink, bias, wb, wo, ln2, w1, w2):
    b, s, _ = x3.shape
    tm = TOKENS_MIX
    nblk = tm // CHUNK
    nb = s // CHUNK
    kw = 2 * N_KV * HEAD_DIM
    tiles_per_seq = s // tm
    n_tiles = b * tiles_per_seq

    def attn_tile(t):
        t = jnp.minimum(t, n_tiles - 1)
        return t // tiles_per_seq, t % tiles_per_seq

    def tile(t):
        bi, i = attn_tile(t)
        return bi, i, 0

    def prev(t):
        bi, i = attn_tile(t)
        return bi, jnp.maximum(i * nblk - 1, 0), 0

    def nxt(t):
        bi, i = attn_tile(t)
        return bi, jnp.minimum((i + 1) * nblk, nb - 1), 0

    def out_tile(t):
        t = jnp.maximum(t - 1, 0)
        return t // tiles_per_seq, t % tiles_per_seq, 0

    return pl.pallas_call(
        functools.partial(_attn_ffn_kernel, tiles_per_seq=tiles_per_seq),
        grid=(n_tiles + 1,),
        in_specs=[
            pl.BlockSpec((None, tm, D_MODEL), tile),
            pl.BlockSpec((None, tm, D_MODEL), tile),
            pl.BlockSpec((None, tm, D_MODEL), tile),
            pl.BlockSpec((None, CHUNK, kw), prev),
            pl.BlockSpec((None, tm, kw), tile),
            pl.BlockSpec((None, CHUNK, kw), nxt),
            _const_spec((D_MODEL, 2 * D_MODEL), layer),
            _const_spec((1, D_MODEL), layer),
            _const_spec((N_HEADS, 1, LANES), layer),
            _const_spec((N_HEADS, CHUNK, 3 * CHUNK)),
            _const_spec((D_MODEL, D_MODEL), layer),
            _const_spec((D_MODEL, D_MODEL), layer),
            _const_spec((1, D_MODEL), layer),
            _const_spec((D_MODEL, D_FF), layer),
            _const_spec((D_FF, D_MODEL), layer),
        ],
        out_specs=pl.BlockSpec((None, tm, D_MODEL), out_tile),
        out_shape=jax.ShapeDtypeStruct((b, s, D_MODEL), jnp.float32),
        scratch_shapes=[pltpu.VMEM((tm, D_MODEL), jnp.bfloat16),
                        pltpu.VMEM((tm, D_MODEL), jnp.float32),
                        pltpu.VMEM((tm, D_MODEL), jnp.float32),
                        pltpu.VMEM((tm, D_MODEL), jnp.bfloat16)],
        compiler_params=pltpu.CompilerParams(
            dimension_semantics=("arbitrary",), vmem_limit_bytes=VMEM_LIMIT_BYTES),
        name="attn_ffn",
    )(x3, h3, ya3, kv3, kv3, kv3, w_b, gq, sink, bias, wb, wo, ln2, w1, w2)


def _alibi_bias():
    qi = np.arange(CHUNK)[:, None]
    kj = np.arange(3 * CHUNK)[None, :]
    rel = np.abs(kj - CHUNK - qi).astype(np.float64)
    slopes = np.asarray(_SLOPES, np.float64)[:, None, None] * LOG2E
    bias = np.where(rel <= WINDOW, -slopes * rel, MASKED_LOGIT)
    return jnp.asarray(bias, jnp.float32)


def kernel(x, ln1_g, w_in, a_norm_g, a_w_s, a_b_s, b_q_norm_g, b_k_norm_g, b_sink,
           w_branch_a, w_branch_b, w_out, ln2_g, w_ff1, w_ff2):
    b, s, d = x.shape
    depth = w_in.shape[0]
    assert d == D_MODEL and s % TOKENS_MIX == 0 and (b * s) % TOKENS_GMLP == 0
    bf = jnp.bfloat16
    o_ga, o_gb, o_za, o_q, o_k, o_v = 0, D_MODEL, 2 * D_MODEL, 4 * D_MODEL, 5 * D_MODEL, 5 * D_MODEL + N_KV * HEAD_DIM

    w_a = jnp.concatenate([w_in[:, :, o_ga:o_gb], w_in[:, :, o_za:o_q]], axis=-1).astype(bf)
    src = np.asarray(_HEAD_ORDER)[np.arange(D_MODEL) // HEAD_DIM] * HEAD_DIM + np.arange(D_MODEL) % HEAD_DIM
    perm = jnp.asarray(np.arange(D_MODEL)[:, None] == src[None, :], bf)
    w_q = jnp.einsum("ldk,kn->ldn", w_in[:, :, o_q:o_k].astype(bf), perm, preferred_element_type=bf)
    w_b = jnp.concatenate([w_in[:, :, o_gb:o_za].astype(bf), w_q], axis=-1)
    w_kv = w_in[:, :, o_k:].astype(bf)
    gk = jnp.tile(b_k_norm_g, (1, N_KV))[:, None, :]
    gq = (jnp.tile(b_q_norm_g, (1, N_HEADS)) * (HEAD_DIM ** -0.5 * LOG2E))[:, None, :]
    sink = jnp.broadcast_to((b_sink * LOG2E)[:, :, None, None], (depth, N_HEADS, 1, LANES))
    ws = (0.5 * a_w_s).astype(bf)
    bs = jnp.broadcast_to(0.5 * a_b_s[:, :, :, None], (depth, A_GROUPS, CHUNK, LANES))
    wa, wo = w_branch_a.astype(bf), w_out.astype(bf)
    wb = jnp.einsum("kn,lkd->lnd", perm, w_branch_b.astype(bf), preferred_element_type=bf)
    w1, w2 = w_ff1.astype(bf), w_ff2.astype(bf)
    bias = _alibi_bias()
    ln1, ln2, gv = ln1_g[:, None, :], ln2_g[:, None, :], a_norm_g[:, None, :]

    x2 = x.reshape(b * s, d)
    for l in range(depth):
        ya, h, kv = _gmlp_kv(l, x2, ln1, w_a, w_kv, gk, gv, ws, bs, wa)
        x3 = _attn_ffn(l, x2.reshape(b, s, d), h.reshape(b, s, d), ya.reshape(b, s, d),
                       kv.reshape(b, s, -1), w_b, gq, sink, bias, wb, wo, ln2, w1, w2)
        x2 = x3.reshape(b * s, d)
    return x2.reshape(b, s, d)
```

```python
import functools

import jax
import jax.numpy as jnp
import numpy as np
from jax import lax
from jax.experimental import pallas as pl
from jax.experimental.pallas import tpu as pltpu

D_MODEL = 1024
CHUNK = 128
A_GROUPS = 8
HEAD_DIM = 64
N_HEADS = 16
N_KV = 4
GQ = N_HEADS // N_KV
WINDOW = 128
D_FF = 4 * D_MODEL
EPS = 1e-6
LANES = 128
MASKED_LOGIT = -1e30
LOG2E = float(np.log2(np.e))

VMEM_LIMIT_BYTES = 56 * 1024 * 1024

TOKENS_GMLP = 1024
TOKENS_MIX = 512
GATE_CHUNKS = 4
FF_CHUNK = 512
MLP_PIECE_STEPS = (0, 1, 2, 4, 5, 6, 8, 9, 10, 12, 13, 14)

_COL_GATE_A, _COL_GATE_B, _COL_UV, _COL_Q, _COL_KV = 0, D_MODEL, 2 * D_MODEL, 4 * D_MODEL, 5 * D_MODEL

_SLOPES = [float(2.0 ** (-8.0 * (h + 1.0) / N_HEADS)) for h in range(N_HEADS)]

_PAIR_HEADS = [(2 * GQ * (pp // GQ) + pp % GQ, 2 * GQ * (pp // GQ) + GQ + pp % GQ)
               for pp in range(N_HEADS // 2)]
_HEAD_ORDER = [h for pair in _PAIR_HEADS for h in pair]


def _const_spec(shape, layer=None, col_block=0):
    nd = len(shape)
    if layer is None:
        return pl.BlockSpec(shape, lambda *_: (0,) * nd, pipeline_mode=pl.Buffered(1))
    return pl.BlockSpec((None, *shape), lambda *_: (layer,) + (0,) * (nd - 1) + (col_block,),
                        pipeline_mode=pl.Buffered(1))


def _rmsnorm_rows(x, g, eps=EPS):
    ms = jnp.mean(x * x, axis=-1, keepdims=True)
    return x * lax.rsqrt(ms + eps) * g


def _gelu_twice(x):
    return x + x * lax.erf(x * np.float32(1.0 / np.sqrt(2.0)))


def _dot(a, b):
    return jnp.dot(a, b, preferred_element_type=jnp.float32)


def _head_pair_sumsq(t, low):
    t2 = t * t
    ssq_all = jnp.sum(t2, axis=-1, keepdims=True)
    ssq_low = jnp.sum(jnp.where(low, t2, 0.0), axis=-1, keepdims=True)
    return jnp.where(low, ssq_low, ssq_all - ssq_low)


_GMLP_ORDER = [("norm", 0), ("v", 0), ("gate", 0), ("u", 0), ("kv", 0),
               ("norm", 1), ("v", 1), ("gate", 1), ("u", 1), ("spatial", 0), ("kv", 1), ("mix", 0),
               ("spatial", 1), ("mix", 1)]


def _gmlp_kv_kernel(x_ref, ln_ref, wga_ref, wuv_ref, wkv_ref, gk_ref, gv_ref, ws_ref, bs_ref, wa_ref,
                    ya_ref, h_ref, kv_ref, gated_ref):
    half = x_ref.shape[0] // 2
    kw = N_KV * HEAD_DIM
    low = lax.broadcasted_iota(jnp.int32, (half, LANES), 1) < HEAD_DIM

    st = [{}, {}]

    def norm(r):
        rows = pl.ds(r * half, half)
        st[r]["h"] = _rmsnorm_rows(x_ref[rows, :], ln_ref[...]).astype(jnp.bfloat16)
        h_ref[rows, :] = st[r]["h"]

    def proj_v(r):
        v2 = _gelu_twice(_dot(st[r]["h"], wuv_ref[:, D_MODEL:]))
        st[r]["vn"] = _rmsnorm_rows(v2, gv_ref[...], eps=4.0 * EPS).astype(jnp.bfloat16)

    def proj_gate(r):
        st[r]["ga"] = _dot(st[r]["h"], wga_ref[...])

    def proj_u(r):
        st[r]["u2"] = _gelu_twice(_dot(st[r]["h"], wuv_ref[:, :D_MODEL]))

    def proj_kv(r):
        kv = _dot(st[r]["h"], wkv_ref[...])
        parts = []
        for pp in range(kw // LANES):
            t = kv[:, pp * LANES:(pp + 1) * LANES]
            ssq = _head_pair_sumsq(t, low)
            parts.append(t * lax.rsqrt(ssq * (1.0 / HEAD_DIM) + EPS) * gk_ref[:, pp * LANES:(pp + 1) * LANES])
        kv_ref[pl.ds(r * half, half), :] = jnp.concatenate(parts + [kv[:, kw:]], axis=-1).astype(jnp.bfloat16)

    def spatial(r):
        base = r * half
        vn, u2 = st[r]["vn"], st[r]["u2"]
        for n in range(half // CHUNK):
            r0 = n * CHUNK
            for g in range(A_GROUPS):
                c0 = g * LANES
                s = _dot(ws_ref[g], vn[r0:r0 + CHUNK, c0:c0 + LANES]) + bs_ref[g]
                gated_ref[base + r0:base + r0 + CHUNK, c0:c0 + LANES] = (
                    u2[r0:r0 + CHUNK, c0:c0 + LANES] * s).astype(jnp.bfloat16)

    def mix(r):
        rows = pl.ds(r * half, half)
        ya = _dot(gated_ref[rows, :], wa_ref[...])
        ya_ref[rows, :] = (jax.nn.sigmoid(st[r]["ga"]) * ya).astype(jnp.bfloat16)

    for stage, r in _GMLP_ORDER:
        {"norm": norm, "v": proj_v, "gate": proj_gate, "u": proj_u, "kv": proj_kv,
         "spatial": spatial, "mix": mix}[stage](r)


def _gmlp_kv(layer, x2, ln_g, w_in, gk, gv, ws, bs, wa):
    t = x2.shape[0]
    tm = TOKENS_GMLP
    kw = N_KV * HEAD_DIM
    rows = lambda width: pl.BlockSpec((tm, width), lambda i: (i, 0))
    return pl.pallas_call(
        _gmlp_kv_kernel,
        grid=(t // tm,),
        in_specs=[
            rows(D_MODEL),
            _const_spec((1, D_MODEL), layer),
            _const_spec((D_MODEL, D_MODEL), layer, col_block=_COL_GATE_A // D_MODEL),
            _const_spec((D_MODEL, 2 * D_MODEL), layer, col_block=_COL_UV // (2 * D_MODEL)),
            _const_spec((D_MODEL, 2 * kw), layer, col_block=_COL_KV // (2 * kw)),
            _const_spec((1, kw), layer),
            _const_spec((1, D_MODEL), layer),
            _const_spec((A_GROUPS, CHUNK, CHUNK), layer),
            _const_spec((A_GROUPS, CHUNK, LANES), layer),
            _const_spec((D_MODEL, D_MODEL), layer),
        ],
        out_specs=[rows(D_MODEL), rows(D_MODEL), rows(2 * kw)],
        out_shape=[
            jax.ShapeDtypeStruct((t, D_MODEL), jnp.bfloat16),
            jax.ShapeDtypeStruct((t, D_MODEL), jnp.bfloat16),
            jax.ShapeDtypeStruct((t, 2 * kw), jnp.bfloat16),
        ],
        scratch_shapes=[pltpu.VMEM((tm, D_MODEL), jnp.bfloat16)],
        compiler_params=pltpu.CompilerParams(
            dimension_semantics=("arbitrary",), vmem_limit_bytes=VMEM_LIMIT_BYTES),
        name="gmlp_kv",
    )(x2, ln_g, w_in, w_in, w_in, gk, gv, ws, bs, wa)


def _attn_ffn_kernel(x_ref, h_ref, ya_ref, kvp_ref, kvc_ref, kvn_ref,
                     wgb_ref, wq_ref, gq_ref, sink_ref, bias_ref, wb_ref, wo_ref, ln2_ref, w1_ref, w2_ref,
                     o_ref, attn_ref, gate_ref, xmid_ref, hmid_ref, *, tiles_per_seq):
    rows = x_ref.shape[0]
    nblk = rows // CHUNK
    step = pl.program_id(0)
    tile_in_seq = lax.rem(jnp.minimum(step, pl.num_programs(0) - 2), tiles_per_seq)

    @pl.when(step == 0)
    def _():
        xmid_ref[...] = jnp.zeros_like(xmid_ref)

    hmid_ref[...] = _rmsnorm_rows(xmid_ref[...], ln2_ref[...]).astype(jnp.bfloat16)

    mlp = {"act": None}

    def mlp_up(c):
        a = jnp.maximum(_dot(hmid_ref[...], w1_ref[:, c * FF_CHUNK:(c + 1) * FF_CHUNK]), 0.0)
        mlp["act"] = (a * a).astype(jnp.bfloat16)

    def mlp_down(c):
        base = xmid_ref[...] if c == 0 else o_ref[...]
        o_ref[...] = base + _dot(mlp["act"], w2_ref[c * FF_CHUNK:(c + 1) * FF_CHUNK, :])

    mlp_pieces = [(f, c) for c in range(D_FF // FF_CHUNK) for f in (mlp_up, mlp_down)]

    low = lax.broadcasted_iota(jnp.int32, (rows, LANES), 1) < HEAD_DIM
    q_low, q_high = {}, {}

    def project_q(pp0):
        c0 = pp0 * LANES
        q = _dot(h_ref[...], wq_ref[:, c0:c0 + 2 * LANES])
        for pp in (pp0, pp0 + 1):
            t = q[:, (pp - pp0) * LANES:(pp - pp0 + 1) * LANES]
            ssq = _head_pair_sumsq(t, low)
            qn = t * lax.rsqrt(ssq * (1.0 / HEAD_DIM) + EPS) * gq_ref[:, pp * LANES:(pp + 1) * LANES]
            q_low[pp] = jnp.where(low, qn, 0.0).astype(jnp.bfloat16)
            q_high[pp] = jnp.where(low, 0.0, qn).astype(jnp.bfloat16)

    kv_ext = jnp.concatenate([kvp_ref[...], kvc_ref[...], kvn_ref[...]], axis=0)
    k_ext, v_ext = kv_ext[:, :N_KV * HEAD_DIM], kv_ext[:, N_KV * HEAD_DIM:]
    ones = jnp.ones((3 * CHUNK, LANES), jnp.bfloat16)
    low_blk = lax.broadcasted_iota(jnp.int32, (CHUNK, LANES), 1) < HEAD_DIM
    pen_first = jnp.where(tile_in_seq == 0, MASKED_LOGIT, 0.0).astype(jnp.float32)
    pen_last = jnp.where(tile_in_seq == tiles_per_seq - 1, MASKED_LOGIT, 0.0).astype(jnp.float32)

    def scores(jj, pp0):
        r0 = jj * CHUNK
        kv_tile = pp0 // (N_HEADS // N_KV)
        lhs = jnp.concatenate(
            [qh[pp][r0:r0 + CHUNK] for pp in (pp0, pp0 + 1) for qh in (q_low, q_high)], axis=0)
        kd = k_ext[r0:r0 + 3 * CHUNK, kv_tile * LANES:(kv_tile + 1) * LANES]
        return lax.dot_general(lhs, kd, (((1,), (1,)), ((), ())),
                               preferred_element_type=jnp.float32)

    def softmax_numerators(jj, pp0, s):
        ps, sink_terms = [], []
        for hh, head in enumerate(_PAIR_HEADS[pp0] + _PAIR_HEADS[pp0 + 1]):
            sb = []
            for c in range(3):
                t = (s[hh * CHUNK:(hh + 1) * CHUNK, c * LANES:(c + 1) * LANES]
                     + bias_ref[head, :, c * LANES:(c + 1) * LANES])
                if c == 0 and jj == 0:
                    t = t + pen_first
                if c == 2 and jj == nblk - 1:
                    t = t + pen_last
                sb.append(t)
            row_max = jnp.max(jnp.maximum(jnp.maximum(sb[0], sb[1]), sb[2]), axis=-1, keepdims=True)
            sink = sink_ref[head]
            m = jnp.maximum(row_max, sink)
            ps.append(jnp.concatenate([jnp.exp2(t - m).astype(jnp.bfloat16) for t in sb], axis=1))
            sink_terms.append(jnp.exp2(sink - m))
        return jnp.concatenate(ps, axis=0), sink_terms

    def weighted_values(jj, pp0, p, sink_terms):
        r0 = jj * CHUNK
        kv_tile = pp0 // (N_HEADS // N_KV)
        vd = jnp.concatenate(
            [v_ext[r0:r0 + 3 * CHUNK, kv_tile * LANES:(kv_tile + 1) * LANES], ones], axis=1)
        r = _dot(p, vd)
        outs = []
        for hh in range(4):
            rh = r[hh * CHUNK:(hh + 1) * CHUNK]
            outs.append(rh[:, :LANES] / (rh[:, LANES:] + sink_terms[hh]))
        for pr in range(2):
            pair = jnp.where(low_blk, outs[2 * pr], outs[2 * pr + 1])
            c0 = (pp0 + pr) * LANES
            attn_ref[r0:r0 + CHUNK, c0:c0 + LANES] = pair.astype(jnp.bfloat16)

    def next_mlp_piece():
        piece, c = mlp_pieces.pop(0)
        piece(c)

    pp0s = list(range(0, N_HEADS // 2, 2))
    for pp0 in pp0s:
        project_q(pp0)
    next_mlp_piece()
    steps = [(jj, pp0) for jj in range(nblk) for pp0 in pp0s]
    gate_every = len(steps) // GATE_CHUNKS
    gate_cols = D_MODEL // GATE_CHUNKS
    s_next = scores(*steps[0])
    for n, att in enumerate(steps):
        s_cur = s_next
        if n + 1 < len(steps):
            s_next = scores(*steps[n + 1])
        if n % gate_every == gate_every - 1:
            c0 = (n // gate_every) * gate_cols
            gate_ref[:, c0:c0 + gate_cols] = _dot(h_ref[...], wgb_ref[:, c0:c0 + gate_cols])
        if n in MLP_PIECE_STEPS:
            next_mlp_piece()
        p, sink_terms = softmax_numerators(*att, s_cur)
        weighted_values(*att, p, sink_terms)

    yb = _dot(attn_ref[...], wb_ref[...])
    next_mlp_piece()
    mixed = (jax.nn.sigmoid(gate_ref[...]) * yb + ya_ref[...].astype(jnp.float32)).astype(jnp.bfloat16)
    x_mid = x_ref[...] + _dot(mixed, wo_ref[...])
    next_mlp_piece()
    next_mlp_piece()
    assert not mlp_pieces
    xmid_ref[...] = x_mid


def _attn_ffn(layer, x3, h3, ya3, kv3, w_in, w_q, gq, sink, bias, wb, wo, ln2, w1, w2):
    b, s, _ = x3.shape
    tm = TOKENS_MIX
    nblk = tm // CHUNK
    nb = s // CHUNK
    kw = 2 * N_KV * HEAD_DIM
    tiles_per_seq = s // tm
    n_tiles = b * tiles_per_seq

    def attn_tile(t):
        t = jnp.minimum(t, n_tiles - 1)
        return t // tiles_per_seq, t % tiles_per_seq

    def tile(t):
        bi, i = attn_tile(t)
        return bi, i, 0

    def prev(t):
        bi, i = attn_tile(t)
        return bi, jnp.maximum(i * nblk - 1, 0), 0

    def nxt(t):
        bi, i = attn_tile(t)
        return bi, jnp.minimum((i + 1) * nblk, nb - 1), 0

    def out_tile(t):
        t = jnp.maximum(t - 1, 0)
        return t // tiles_per_seq, t % tiles_per_seq, 0

    return pl.pallas_call(
        functools.partial(_attn_ffn_kernel, tiles_per_seq=tiles_per_seq),
        grid=(n_tiles + 1,),
        in_specs=[
            pl.BlockSpec((None, tm, D_MODEL), tile),
            pl.BlockSpec((None, tm, D_MODEL), tile),
            pl.BlockSpec((None, tm, D_MODEL), tile),
            pl.BlockSpec((None, CHUNK, kw), prev),
            pl.BlockSpec((None, tm, kw), tile),
            pl.BlockSpec((None, CHUNK, kw), nxt),
            _const_spec((D_MODEL, D_MODEL), layer, col_block=_COL_GATE_B // D_MODEL),
            _const_spec((D_MODEL, D_MODEL), layer),
            _const_spec((1, D_MODEL), layer),
            _const_spec((N_HEADS, 1, LANES), layer),
            _const_spec((N_HEADS, CHUNK, 3 * CHUNK)),
            _const_spec((D_MODEL, D_MODEL), layer),
            _const_spec((D_MODEL, D_MODEL), layer),
            _const_spec((1, D_MODEL), layer),
            _const_spec((D_MODEL, D_FF), layer),
            _const_spec((D_FF, D_MODEL), layer),
        ],
        out_specs=pl.BlockSpec((None, tm, D_MODEL), out_tile),
        out_shape=jax.ShapeDtypeStruct((b, s, D_MODEL), jnp.float32),
        scratch_shapes=[pltpu.VMEM((tm, D_MODEL), jnp.bfloat16),
                        pltpu.VMEM((tm, D_MODEL), jnp.float32),
                        pltpu.VMEM((tm, D_MODEL), jnp.float32),
                        pltpu.VMEM((tm, D_MODEL), jnp.bfloat16)],
        compiler_params=pltpu.CompilerParams(
            dimension_semantics=("arbitrary",), vmem_limit_bytes=VMEM_LIMIT_BYTES),
        name="attn_ffn",
    )(x3, h3, ya3, kv3, kv3, kv3, w_in, w_q, gq, sink, bias, wb, wo, ln2, w1, w2)


def _alibi_bias():
    qi = np.arange(CHUNK)[:, None]
    kj = np.arange(3 * CHUNK)[None, :]
    rel = np.abs(kj - CHUNK - qi).astype(np.float64)
    slopes = np.asarray(_SLOPES, np.float64)[:, None, None] * LOG2E
    bias = np.where(rel <= WINDOW, -slopes * rel, MASKED_LOGIT)
    return jnp.asarray(bias, jnp.float32)


def kernel(x, ln1_g, w_in, a_norm_g, a_w_s, a_b_s, b_q_norm_g, b_k_norm_g, b_sink,
           w_branch_a, w_branch_b, w_out, ln2_g, w_ff1, w_ff2):
    b, s, d = x.shape
    depth = w_in.shape[0]
    assert d == D_MODEL and s % TOKENS_MIX == 0 and (b * s) % TOKENS_GMLP == 0
    bf = jnp.bfloat16

    w_in = w_in.astype(bf)
    src = np.asarray(_HEAD_ORDER)[np.arange(D_MODEL) // HEAD_DIM] * HEAD_DIM + np.arange(D_MODEL) % HEAD_DIM
    perm = jnp.asarray(np.arange(D_MODEL)[:, None] == src[None, :], bf)
    w_q = jnp.einsum("ldk,kn->ldn", w_in[:, :, _COL_Q:_COL_KV], perm, preferred_element_type=bf)
    gk = jnp.tile(b_k_norm_g, (1, N_KV))[:, None, :]
    gq = (jnp.tile(b_q_norm_g, (1, N_HEADS)) * (HEAD_DIM ** -0.5 * LOG2E))[:, None, :]
    sink = jnp.broadcast_to((b_sink * LOG2E)[:, :, None, None], (depth, N_HEADS, 1, LANES))
    ws = (0.5 * a_w_s).astype(bf)
    bs = jnp.broadcast_to(0.5 * a_b_s[:, :, :, None], (depth, A_GROUPS, CHUNK, LANES))
    wa, wo = w_branch_a.astype(bf), w_out.astype(bf)
    wb = jnp.einsum("kn,lkd->lnd", perm, w_branch_b.astype(bf), preferred_element_type=bf)
    w1, w2 = w_ff1.astype(bf), w_ff2.astype(bf)
    bias = _alibi_bias()
    ln1, ln2, gv = ln1_g[:, None, :], ln2_g[:, None, :], a_norm_g[:, None, :]

    x2 = x.reshape(b * s, d)
    for l in range(depth):
        ya, h, kv = _gmlp_kv(l, x2, ln1, w_in, gk, gv, ws, bs, wa)
        x3 = _attn_ffn(l, x2.reshape(b, s, d), h.reshape(b, s, d), ya.reshape(b, s, d),
                       kv.reshape(b, s, -1), w_in, w_q, gq, sink, bias, wb, wo, ln2, w1, w2)
        x2 = x3.reshape(b * s, d)
    return x2.reshape(b, s, d)
```

```python
import functools

import jax
import jax.numpy as jnp
import numpy as np
from jax import lax
from jax.experimental import pallas as pl
from jax.experimental.pallas import tpu as pltpu

D_MODEL = 1024
CHUNK = 128
A_GROUPS = 8
HEAD_DIM = 64
N_HEADS = 16
N_KV = 4
GQ = N_HEADS // N_KV
WINDOW = 128
D_FF = 4 * D_MODEL
EPS = 1e-6
LANES = 128
MASKED_LOGIT = -1e30
LOG2E = float(np.log2(np.e))

VMEM_LIMIT_BYTES = 56 * 1024 * 1024

TOKENS_GMLP = 1024
TOKENS_MIX = 512
GATE_CHUNKS = 4
FF_CHUNK = 512
MLP_PIECE_STEPS = (1, 2, 3, 5, 6, 7, 9, 10, 11, 13, 14, 15)

_COL_GATE_A, _COL_GATE_B, _COL_UV, _COL_Q, _COL_KV = 0, D_MODEL, 2 * D_MODEL, 4 * D_MODEL, 5 * D_MODEL

_SLOPES = [float(2.0 ** (-8.0 * (h + 1.0) / N_HEADS)) for h in range(N_HEADS)]

_PAIR_HEADS = [(2 * GQ * (pp // GQ) + pp % GQ, 2 * GQ * (pp // GQ) + GQ + pp % GQ)
               for pp in range(N_HEADS // 2)]
_HEAD_ORDER = [h for pair in _PAIR_HEADS for h in pair]


def _const_spec(shape, layer=None, col_block=0):
    nd = len(shape)
    if layer is None:
        return pl.BlockSpec(shape, lambda *_: (0,) * nd, pipeline_mode=pl.Buffered(1))
    return pl.BlockSpec((None, *shape), lambda *_: (layer,) + (0,) * (nd - 1) + (col_block,),
                        pipeline_mode=pl.Buffered(1))


def _rmsnorm_rows(x, g, eps=EPS):
    ms = jnp.mean(x * x, axis=-1, keepdims=True)
    return x * lax.rsqrt(ms + eps) * g


def _gelu_twice(x):
    return x + x * lax.erf(x * np.float32(1.0 / np.sqrt(2.0)))


def _dot(a, b):
    return jnp.dot(a, b, preferred_element_type=jnp.float32)


def _head_pair_sumsq(t, low):
    t2 = t * t
    ssq_all = jnp.sum(t2, axis=-1, keepdims=True)
    ssq_low = jnp.sum(jnp.where(low, t2, 0.0), axis=-1, keepdims=True)
    return jnp.where(low, ssq_low, ssq_all - ssq_low)


_GMLP_ORDER = [("norm", 0), ("v", 0), ("gate", 0), ("u", 0), ("kv", 0),
               ("norm", 1), ("v", 1), ("gate", 1), ("u", 1), ("spatial", 0), ("kv", 1), ("mix", 0),
               ("spatial", 1), ("mix", 1)]


def _gmlp_kv_kernel(x_ref, ln_ref, wga_ref, wuv_ref, wkv_ref, gk_ref, gv_ref, ws_ref, bs_ref, wa_ref,
                    ya_ref, h_ref, kv_ref, gated_ref):
    half = x_ref.shape[0] // 2
    kw = N_KV * HEAD_DIM
    low = lax.broadcasted_iota(jnp.int32, (half, LANES), 1) < HEAD_DIM

    st = [{}, {}]

    def norm(r):
        rows = pl.ds(r * half, half)
        st[r]["h"] = _rmsnorm_rows(x_ref[rows, :], ln_ref[...]).astype(jnp.bfloat16)
        h_ref[rows, :] = st[r]["h"]

    def proj_v(r):
        v2 = _gelu_twice(_dot(st[r]["h"], wuv_ref[:, D_MODEL:]))
        st[r]["vn"] = _rmsnorm_rows(v2, gv_ref[...], eps=4.0 * EPS).astype(jnp.bfloat16)

    def proj_gate(r):
        st[r]["ga"] = _dot(st[r]["h"], wga_ref[...])

    def proj_u(r):
        st[r]["u2"] = _gelu_twice(_dot(st[r]["h"], wuv_ref[:, :D_MODEL]))

    def proj_kv(r):
        kv = _dot(st[r]["h"], wkv_ref[...])
        parts = []
        for pp in range(kw // LANES):
            t = kv[:, pp * LANES:(pp + 1) * LANES]
            ssq = _head_pair_sumsq(t, low)
            parts.append(t * lax.rsqrt(ssq * (1.0 / HEAD_DIM) + EPS) * gk_ref[:, pp * LANES:(pp + 1) * LANES])
        kv_ref[pl.ds(r * half, half), :] = jnp.concatenate(parts + [kv[:, kw:]], axis=-1).astype(jnp.bfloat16)

    def spatial(r):
        base = r * half
        vn, u2 = st[r]["vn"], st[r]["u2"]
        for n in range(half // CHUNK):
            r0 = n * CHUNK
            for g in range(A_GROUPS):
                c0 = g * LANES
                s = _dot(ws_ref[g], vn[r0:r0 + CHUNK, c0:c0 + LANES]) + bs_ref[g]
                gated_ref[base + r0:base + r0 + CHUNK, c0:c0 + LANES] = (
                    u2[r0:r0 + CHUNK, c0:c0 + LANES] * s).astype(jnp.bfloat16)

    def mix(r):
        rows = pl.ds(r * half, half)
        ya = _dot(gated_ref[rows, :], wa_ref[...])
        ya_ref[rows, :] = (jax.nn.sigmoid(st[r]["ga"]) * ya).astype(jnp.bfloat16)

    for stage, r in _GMLP_ORDER:
        {"norm": norm, "v": proj_v, "gate": proj_gate, "u": proj_u, "kv": proj_kv,
         "spatial": spatial, "mix": mix}[stage](r)


def _gmlp_kv(layer, x2, ln_g, w_in, gk, gv, ws, bs, wa):
    t = x2.shape[0]
    tm = TOKENS_GMLP
    kw = N_KV * HEAD_DIM
    rows = lambda width: pl.BlockSpec((tm, width), lambda i: (i, 0))
    return pl.pallas_call(
        _gmlp_kv_kernel,
        grid=(t // tm,),
        in_specs=[
            rows(D_MODEL),
            _const_spec((1, D_MODEL), layer),
            _const_spec((D_MODEL, D_MODEL), layer, col_block=_COL_GATE_A // D_MODEL),
            _const_spec((D_MODEL, 2 * D_MODEL), layer, col_block=_COL_UV // (2 * D_MODEL)),
            _const_spec((D_MODEL, 2 * kw), layer, col_block=_COL_KV // (2 * kw)),
            _const_spec((1, kw), layer),
            _const_spec((1, D_MODEL), layer),
            _const_spec((A_GROUPS, CHUNK, CHUNK), layer),
            _const_spec((A_GROUPS, CHUNK, LANES), layer),
            _const_spec((D_MODEL, D_MODEL), layer),
        ],
        out_specs=[rows(D_MODEL), rows(D_MODEL), rows(2 * kw)],
        out_shape=[
            jax.ShapeDtypeStruct((t, D_MODEL), jnp.bfloat16),
            jax.ShapeDtypeStruct((t, D_MODEL), jnp.bfloat16),
            jax.ShapeDtypeStruct((t, 2 * kw), jnp.bfloat16),
        ],
        scratch_shapes=[pltpu.VMEM((tm, D_MODEL), jnp.bfloat16)],
        compiler_params=pltpu.CompilerParams(
            dimension_semantics=("arbitrary",), vmem_limit_bytes=VMEM_LIMIT_BYTES),
        name="gmlp_kv",
    )(x2, ln_g, w_in, w_in, w_in, gk, gv, ws, bs, wa)


def _attn_ffn_kernel(x_ref, h_ref, ya_ref, kvp_ref, kvc_ref, kvn_ref,
                     wgb_ref, wq_ref, gq_ref, sink_ref, bias_ref, wb_ref, wo_ref, ln2_ref, w1_ref, w2_ref,
                     o_ref, attn_ref, gate_ref, xmid_ref, hmid_ref, *, tiles_per_seq):
    rows = x_ref.shape[0]
    nblk = rows // CHUNK
    step = pl.program_id(0)
    tile_in_seq = lax.rem(jnp.minimum(step, pl.num_programs(0) - 2), tiles_per_seq)

    @pl.when(step == 0)
    def _():
        xmid_ref[...] = jnp.zeros_like(xmid_ref)

    hmid_ref[...] = _rmsnorm_rows(xmid_ref[...], ln2_ref[...]).astype(jnp.bfloat16)

    mlp = {"act": None}

    def mlp_up(c):
        a = jnp.maximum(_dot(hmid_ref[...], w1_ref[:, c * FF_CHUNK:(c + 1) * FF_CHUNK]), 0.0)
        mlp["act"] = (a * a).astype(jnp.bfloat16)

    def mlp_down(c):
        base = xmid_ref[...] if c == 0 else o_ref[...]
        o_ref[...] = base + _dot(mlp["act"], w2_ref[c * FF_CHUNK:(c + 1) * FF_CHUNK, :])

    mlp_pieces = [(f, c) for c in range(D_FF // FF_CHUNK) for f in (mlp_up, mlp_down)]

    low = lax.broadcasted_iota(jnp.int32, (rows, LANES), 1) < HEAD_DIM
    q_low, q_high = {}, {}

    def project_q(pp0):
        c0 = pp0 * LANES
        q = _dot(h_ref[...], wq_ref[:, c0:c0 + 2 * LANES])
        for pp in (pp0, pp0 + 1):
            t = q[:, (pp - pp0) * LANES:(pp - pp0 + 1) * LANES]
            ssq = _head_pair_sumsq(t, low)
            qn = t * lax.rsqrt(ssq * (1.0 / HEAD_DIM) + EPS) * gq_ref[:, pp * LANES:(pp + 1) * LANES]
            q_low[pp] = jnp.where(low, qn, 0.0).astype(jnp.bfloat16)
            q_high[pp] = jnp.where(low, 0.0, qn).astype(jnp.bfloat16)

    kv_ext = jnp.concatenate([kvp_ref[...], kvc_ref[...], kvn_ref[...]], axis=0)
    k_ext, v_ext = kv_ext[:, :N_KV * HEAD_DIM], kv_ext[:, N_KV * HEAD_DIM:]
    ones = jnp.ones((3 * CHUNK, LANES), jnp.bfloat16)
    low_blk = lax.broadcasted_iota(jnp.int32, (CHUNK, LANES), 1) < HEAD_DIM
    pen_first = jnp.where(tile_in_seq == 0, MASKED_LOGIT, 0.0).astype(jnp.float32)
    pen_last = jnp.where(tile_in_seq == tiles_per_seq - 1, MASKED_LOGIT, 0.0).astype(jnp.float32)

    def scores(jj, pp0):
        r0 = jj * CHUNK
        kv_tile = pp0 // (N_HEADS // N_KV)
        lhs = jnp.concatenate(
            [qh[pp][r0:r0 + CHUNK] for pp in (pp0, pp0 + 1) for qh in (q_low, q_high)], axis=0)
        kd = k_ext[r0:r0 + 3 * CHUNK, kv_tile * LANES:(kv_tile + 1) * LANES]
        return lax.dot_general(lhs, kd, (((1,), (1,)), ((), ())),
                               preferred_element_type=jnp.float32)

    def softmax_numerators(jj, pp0, s):
        ps, sink_terms = [], []
        for hh, head in enumerate(_PAIR_HEADS[pp0] + _PAIR_HEADS[pp0 + 1]):
            sb = []
            for c in range(3):
                t = (s[hh * CHUNK:(hh + 1) * CHUNK, c * LANES:(c + 1) * LANES]
                     + bias_ref[head, :, c * LANES:(c + 1) * LANES])
                if c == 0 and jj == 0:
                    t = t + pen_first
                if c == 2 and jj == nblk - 1:
                    t = t + pen_last
                sb.append(t)
            row_max = jnp.max(jnp.maximum(jnp.maximum(sb[0], sb[1]), sb[2]), axis=-1, keepdims=True)
            sink = sink_ref[head]
            m = jnp.maximum(row_max, sink)
            ps.append(jnp.concatenate([jnp.exp2(t - m).astype(jnp.bfloat16) for t in sb], axis=1))
            sink_terms.append(jnp.exp2(sink - m))
        return jnp.concatenate(ps, axis=0), sink_terms

    def weighted_values(jj, pp0, p, sink_terms):
        r0 = jj * CHUNK
        kv_tile = pp0 // (N_HEADS // N_KV)
        vd = jnp.concatenate(
            [v_ext[r0:r0 + 3 * CHUNK, kv_tile * LANES:(kv_tile + 1) * LANES], ones], axis=1)
        r = _dot(p, vd)
        outs = []
        for hh in range(4):
            rh = r[hh * CHUNK:(hh + 1) * CHUNK]
            outs.append(rh[:, :LANES] / (rh[:, LANES:] + sink_terms[hh]))
        for pr in range(2):
            pair = jnp.where(low_blk, outs[2 * pr], outs[2 * pr + 1])
            c0 = (pp0 + pr) * LANES
            attn_ref[r0:r0 + CHUNK, c0:c0 + LANES] = pair.astype(jnp.bfloat16)

    def next_mlp_piece():
        piece, c = mlp_pieces.pop(0)
        piece(c)

    pp0s = list(range(0, N_HEADS // 2, 2))
    for pp0 in pp0s:
        project_q(pp0)
    next_mlp_piece()
    steps = [(jj, pp0) for jj in range(nblk) for pp0 in pp0s]
    gate_every = len(steps) // GATE_CHUNKS
    gate_cols = D_MODEL // GATE_CHUNKS
    s_next = scores(*steps[0])
    for n, att in enumerate(steps):
        s_cur = s_next
        if n + 1 < len(steps):
            s_next = scores(*steps[n + 1])
        if n % gate_every == 0:
            c0 = (n // gate_every) * gate_cols
            gate_ref[:, c0:c0 + gate_cols] = _dot(h_ref[...], wgb_ref[:, c0:c0 + gate_cols])
        if n in MLP_PIECE_STEPS:
            next_mlp_piece()
        p, sink_terms = softmax_numerators(*att, s_cur)
        weighted_values(*att, p, sink_terms)

    yb = _dot(attn_ref[...], wb_ref[...])
    next_mlp_piece()
    mixed = (jax.nn.sigmoid(gate_ref[...]) * yb + ya_ref[...].astype(jnp.float32)).astype(jnp.bfloat16)
    x_mid = x_ref[...] + _dot(mixed, wo_ref[...])
    next_mlp_piece()
    next_mlp_piece()
    assert not mlp_pieces
    xmid_ref[...] = x_mid


def _attn_ffn(layer, x3, h3, ya3, kv3, w_in, w_q, gq, sink, bias, wb, wo, ln2, w1, w2):
    b, s, _ = x3.shape
    tm = TOKENS_MIX
    nblk = tm // CHUNK
    nb = s // CHUNK
    kw = 2 * N_KV * HEAD_DIM
    tiles_per_seq = s // tm
    n_tiles = b * tiles_per_seq

    def attn_tile(t):
        t = jnp.minimum(t, n_tiles - 1)
        return t // tiles_per_seq, t % tiles_per_seq

    def tile(t):
        bi, i = attn_tile(t)
        return bi, i, 0

    def prev(t):
        bi, i = attn_tile(t)
        return bi, jnp.maximum(i * nblk - 1, 0), 0

    def nxt(t):
        bi, i = attn_tile(t)
        return bi, jnp.minimum((i + 1) * nblk, nb - 1), 0

    def out_tile(t):
        t = jnp.maximum(t - 1, 0)
        return t // tiles_per_seq, t % tiles_per_seq, 0

    return pl.pallas_call(
        functools.partial(_attn_ffn_kernel, tiles_per_seq=tiles_per_seq),
        grid=(n_tiles + 1,),
        in_specs=[
            pl.BlockSpec((None, tm, D_MODEL), tile),
            pl.BlockSpec((None, tm, D_MODEL), tile),
            pl.BlockSpec((None, tm, D_MODEL), tile),
            pl.BlockSpec((None, CHUNK, kw), prev),
            pl.BlockSpec((None, tm, kw), tile),
            pl.BlockSpec((None, CHUNK, kw), nxt),
            _const_spec((D_MODEL, D_MODEL), layer, col_block=_COL_GATE_B // D_MODEL),
            _const_spec((D_MODEL, D_MODEL), layer),
            _const_spec((1, D_MODEL), layer),
            _const_spec((N_HEADS, 1, LANES), layer),
            _const_spec((N_HEADS, CHUNK, 3 * CHUNK)),
            _const_spec((D_MODEL, D_MODEL), layer),
            _const_spec((D_MODEL, D_MODEL), layer),
            _const_spec((1, D_MODEL), layer),
            _const_spec((D_MODEL, D_FF), layer),
            _const_spec((D_FF, D_MODEL), layer),
        ],
        out_specs=pl.BlockSpec((None, tm, D_MODEL), out_tile),
        out_shape=jax.ShapeDtypeStruct((b, s, D_MODEL), jnp.float32),
        scratch_shapes=[pltpu.VMEM((tm, D_MODEL), jnp.bfloat16),
                        pltpu.VMEM((tm, D_MODEL), jnp.float32),
                        pltpu.VMEM((tm, D_MODEL), jnp.float32),
                        pltpu.VMEM((tm, D_MODEL), jnp.bfloat16)],
        compiler_params=pltpu.CompilerParams(
            dimension_semantics=("arbitrary",), vmem_limit_bytes=VMEM_LIMIT_BYTES),
        name="attn_ffn",
    )(x3, h3, ya3, kv3, kv3, kv3, w_in, w_q, gq, sink, bias, wb, wo, ln2, w1, w2)


def _alibi_bias():
    qi = np.arange(CHUNK)[:, None]
    kj = np.arange(3 * CHUNK)[None, :]
    rel = np.abs(kj - CHUNK - qi).astype(np.float64)
    slopes = np.asarray(_SLOPES, np.float64)[:, None, None] * LOG2E
    bias = np.where(rel <= WINDOW, -slopes * rel, MASKED_LOGIT)
    return jnp.asarray(bias, jnp.float32)


def kernel(x, ln1_g, w_in, a_norm_g, a_w_s, a_b_s, b_q_norm_g, b_k_norm_g, b_sink,
           w_branch_a, w_branch_b, w_out, ln2_g, w_ff1, w_ff2):
    b, s, d = x.shape
    depth = w_in.shape[0]
    assert d == D_MODEL and s % TOKENS_MIX == 0 and (b * s) % TOKENS_GMLP == 0
    bf = jnp.bfloat16

    w_in = w_in.astype(bf)
    src = np.asarray(_HEAD_ORDER)[np.arange(D_MODEL) // HEAD_DIM] * HEAD_DIM + np.arange(D_MODEL) % HEAD_DIM
    perm = jnp.asarray(np.arange(D_MODEL)[:, None] == src[None, :], bf)
    w_q = jnp.einsum("ldk,kn->ldn", w_in[:, :, _COL_Q:_COL_KV], perm, preferred_element_type=bf)
    gk = jnp.tile(b_k_norm_g, (1, N_KV))[:, None, :]
    gq = (jnp.tile(b_q_norm_g, (1, N_HEADS)) * (HEAD_DIM ** -0.5 * LOG2E))[:, None, :]
    sink = jnp.broadcast_to((b_sink * LOG2E)[:, :, None, None], (depth, N_HEADS, 1, LANES))
    ws = (0.5 * a_w_s).astype(bf)
    bs = jnp.broadcast_to(0.5 * a_b_s[:, :, :, None], (depth, A_GROUPS, CHUNK, LANES))
    wa, wo = w_branch_a.astype(bf), w_out.astype(bf)
    wb = jnp.einsum("kn,lkd->lnd", perm, w_branch_b.astype(bf), preferred_element_type=bf)
    w1, w2 = w_ff1.astype(bf), w_ff2.astype(bf)
    bias = _alibi_bias()
    ln1, ln2, gv = ln1_g[:, None, :], ln2_g[:, None, :], a_norm_g[:, None, :]

    x2 = x.reshape(b * s, d)
    for l in range(depth):
        ya, h, kv = _gmlp_kv(l, x2, ln1, w_in, gk, gv, ws, bs, wa)
        x3 = _attn_ffn(l, x2.reshape(b, s, d), h.reshape(b, s, d), ya.reshape(b, s, d),
                       kv.reshape(b, s, -1), w_in, w_q, gq, sink, bias, wb, wo, ln2, w1, w2)
        x2 = x3.reshape(b * s, d)
    return x2.reshape(b, s, d)
```
